```python
import functools
import jax, jax.numpy as jnp
from jax import lax
import numpy as np

D_MODEL = 4096
BATCH = 2
SEQ = 4096
DEPTH = 1
DEC_BATCH = 128
DEC_SEQ = 1
PAST_LEN = 8192
PAGE_SIZE = 128

CONV_WIDTH = D_MODEL // 2
CONV_K = 31
HEAD_DIM = 128
N_HEADS = (D_MODEL // 2) // HEAD_DIM
N_KV_HEADS = N_HEADS // 4
GROUP = N_HEADS // N_KV_HEADS
ROT_DIM = HEAD_DIM // 4
ROPE_THETA = 500000.0
WINDOW = 128
BLOCK = 128
D_FF = 11008
FFN_CONV_K = 3
EPS = 1e-6
Q_WIDTH = N_HEADS * HEAD_DIM
KV_WIDTH = N_KV_HEADS * HEAD_DIM
A_END = 2 * CONV_WIDTH
Q_END = A_END + Q_WIDTH
K_END = Q_END + KV_WIDTH
V_END = K_END + KV_WIDTH
IN_WIDTH = V_END + 2 * D_MODEL

kernel_name = "hybrid_conformer_swa_sink_convffn_step"


def rms_norm(x, g):
    xf = x.astype(jnp.float32)
    y = xf * lax.rsqrt(jnp.mean(xf * xf, axis=-1, keepdims=True) + EPS)
    return (y * g.astype(jnp.float32)).astype(x.dtype)


def layer_norm(x, g, b):
    xf = x.astype(jnp.float32)
    mu = jnp.mean(xf, axis=-1, keepdims=True)
    xc = xf - mu
    y = xc * lax.rsqrt(jnp.mean(xc * xc, axis=-1, keepdims=True) + EPS)
    return (y * g.astype(jnp.float32) + b.astype(jnp.float32)).astype(x.dtype)


def partial_rope(x, pos):
    half = ROT_DIM // 2
    inv = ROPE_THETA ** (-2.0 * jnp.arange(half, dtype=jnp.float32) / ROT_DIM)
    ang = pos.astype(jnp.float32)[:, None] * inv[None, :]
    cos = jnp.cos(ang)[None, :, None, :]
    sin = jnp.sin(ang)[None, :, None, :]
    xr = x[..., :ROT_DIM].astype(jnp.float32)
    x1, x2 = xr[..., :half], xr[..., half:]
    rot = jnp.concatenate([x1 * cos - x2 * sin, x2 * cos + x1 * sin], axis=-1).astype(x.dtype)
    return jnp.concatenate([rot, x[..., ROT_DIM:]], axis=-1)


def causal_depthwise(hist, x, w, b):
    xx = jnp.concatenate([hist, x], axis=1)
    c = x.shape[-1]
    y = lax.conv_general_dilated(xx, w[:, None, :], window_strides=(1,), padding='VALID',
                                 dimension_numbers=('NWC', 'WIO', 'NWC'), feature_group_count=c)
    return y + b, xx[:, -(w.shape[0] - 1):]


def sink_probs(s, mask, sink):
    s = jnp.where(mask, s, -jnp.inf)
    m = jnp.maximum(jnp.max(s, axis=-1, keepdims=True), sink)
    p = jnp.exp(s - m)
    return p / (jnp.sum(p, axis=-1, keepdims=True) + jnp.exp(sink - m))


def swa_prompt(q, k, v, sinks):
    B, L = q.shape[0], q.shape[1]
    nb = L // BLOCK
    qb = q.reshape(B, nb, BLOCK, N_KV_HEADS, GROUP, HEAD_DIM)
    kb = k.reshape(B, nb, BLOCK, N_KV_HEADS, HEAD_DIM)
    vb = v.reshape(B, nb, BLOCK, N_KV_HEADS, HEAD_DIM)

    def with_prev(t):
        prev = jnp.concatenate([jnp.zeros_like(t[:, :1]), t[:, :-1]], axis=1)
        return jnp.concatenate([prev, t], axis=2)

    kk, vv = with_prev(kb), with_prev(vb)
    s = jnp.einsum('bnqkgd,bnskd->bnkgqs', qb, kk,
                   preferred_element_type=jnp.float32) * (HEAD_DIM ** -0.5)
    qi = jnp.arange(BLOCK) + BLOCK
    ki = jnp.arange(2 * BLOCK)
    diff = qi[:, None] - ki[None, :]
    band = (diff >= 0) & (diff < WINDOW)
    real = (jnp.arange(nb) > 0)[:, None, None] | (ki >= BLOCK)[None, None, :]
    mask = (band[None] & real)[None, :, None, None]
    sink = sinks.astype(jnp.float32).reshape(1, 1, N_KV_HEADS, GROUP, 1, 1)
    p = sink_probs(s, mask, sink)
    o = jnp.einsum('bnkgqs,bnskd->bnqkgd', p.astype(v.dtype), vv)
    return o.reshape(B, L, N_HEADS, HEAD_DIM), k[:, -WINDOW:], v[:, -WINDOW:]


def swa_sample(q, k, v, cache_k, cache_v, sinks):
    DB, T = q.shape[0], q.shape[1]
    W = cache_k.shape[1]
    kk = jnp.concatenate([cache_k, k], axis=1)
    vv = jnp.concatenate([cache_v, v], axis=1)
    qb = q.reshape(DB, T, N_KV_HEADS, GROUP, HEAD_DIM)
    s = jnp.einsum('bqkgd,bskd->bkgqs', qb, kk,
                   preferred_element_type=jnp.float32) * (HEAD_DIM ** -0.5)
    qpos = PAST_LEN + jnp.arange(T)
    kpos = PAST_LEN - W + jnp.arange(W + T)
    diff = qpos[:, None] - kpos[None, :]
    mask = ((diff >= 0) & (diff < WINDOW))[None, None, None]
    sink = sinks.astype(jnp.float32).reshape(1, N_KV_HEADS, GROUP, 1, 1)
    p = sink_probs(s, mask, sink)
    o = jnp.einsum('bkgqs,bskd->bqkgd', p.astype(v.dtype), vv)
    return o.reshape(DB, T, N_HEADS, HEAD_DIM), kk[:, -W:], vv[:, -W:]


def trunk_layer(x, pos, attend, conv_a_hist, ffn_hist, norm1_g, w_in, conv_a_w, conv_a_b,
                ln_a_g, ln_a_b, w_a_out, w_attn_out, w_o, norm2_g, w_up, conv_f_w, conv_f_b, w_down):
    B, L, _ = x.shape
    xn = rms_norm(x, norm1_g)
    z = jnp.einsum('bld,de->ble', xn, w_in)
    a_in, q, k, v, gate = (z[..., :A_END], z[..., A_END:Q_END], z[..., Q_END:K_END],
                           z[..., K_END:V_END], z[..., V_END:])
    u = a_in[..., :CONV_WIDTH] * jax.nn.sigmoid(a_in[..., CONV_WIDTH:])
    c, conv_a_new = causal_depthwise(conv_a_hist, u, conv_a_w, conv_a_b)
    c = jax.nn.silu(layer_norm(c, ln_a_g, ln_a_b))
    br_a = jnp.einsum('blc,cd->bld', c, w_a_out)
    q = partial_rope(q.reshape(B, L, N_HEADS, HEAD_DIM), pos)
    k = partial_rope(k.reshape(B, L, N_KV_HEADS, HEAD_DIM), pos)
    v = v.reshape(B, L, N_KV_HEADS, HEAD_DIM)
    o, k_win, v_win = attend(q, k, v)
    br_b = jnp.einsum('ble,ed->bld', o.reshape(B, L, Q_WIDTH), w_attn_out)
    g = jax.nn.sigmoid(gate)
    merged = g[..., :D_MODEL] * br_a + g[..., D_MODEL:] * br_b
    h = x + jnp.einsum('bld,de->ble', merged, w_o)
    up = jnp.einsum('bld,df->blf', rms_norm(h, norm2_g), w_up)
    up, ffn_new = causal_depthwise(ffn_hist, up, conv_f_w, conv_f_b)
    f = jax.nn.silu(up[..., :D_FF]) * up[..., D_FF:]
    y = h + jnp.einsum('blf,fd->bld', f, w_down)
    return y, k_win, v_win, conv_a_new, ffn_new


def setup_inputs(seed: int = 0) -> dict:
    key = jax.random.key(seed)
    ks = jax.random.split(key, 24)

    def nrm(k, shape, scale):
        return jax.random.normal(k, shape, jnp.float32) * scale

    win = min(WINDOW, PAST_LEN)
    return {
        "x_prompt": nrm(ks[0], (BATCH, SEQ, D_MODEL), 1.0),
        "x_sample": nrm(ks[1], (DEC_BATCH, DEC_SEQ, D_MODEL), 1.0),
        "cache_k_win": nrm(ks[2], (DEPTH, DEC_BATCH, win, N_KV_HEADS, HEAD_DIM), 1.0),
        "cache_v_win": nrm(ks[3], (DEPTH, DEC_BATCH, win, N_KV_HEADS, HEAD_DIM), 1.0),
        "state_conv_a": nrm(ks[4], (DEPTH, DEC_BATCH, CONV_K - 1, CONV_WIDTH), 0.5),
        "state_conv_ffn": nrm(ks[5], (DEPTH, DEC_BATCH, FFN_CONV_K - 1, 2 * D_FF), 1.0),
        "norm1_g": 1.0 + nrm(ks[6], (DEPTH, D_MODEL), 0.02),
        "w_in": nrm(ks[7], (DEPTH, D_MODEL, IN_WIDTH), D_MODEL ** -0.5),
        "conv_a_w": nrm(ks[8], (DEPTH, CONV_K, CONV_WIDTH), CONV_K ** -0.5),
        "conv_a_b": nrm(ks[9], (DEPTH, CONV_WIDTH), 0.02),
        "ln_a_g": 1.0 + nrm(ks[10], (DEPTH, CONV_WIDTH), 0.02),
        "ln_a_b": nrm(ks[11], (DEPTH, CONV_WIDTH), 0.02),
        "w_a_out": nrm(ks[12], (DEPTH, CONV_WIDTH, D_MODEL), CONV_WIDTH ** -0.5),
        "attn_sinks": nrm(ks[13], (DEPTH, N_HEADS), 0.5),
        "w_attn_out": nrm(ks[14], (DEPTH, Q_WIDTH, D_MODEL), Q_WIDTH ** -0.5),
        "w_o": nrm(ks[15], (DEPTH, D_MODEL, D_MODEL), D_MODEL ** -0.5),
        "norm2_g": 1.0 + nrm(ks[16], (DEPTH, D_MODEL), 0.02),
        "w_up": nrm(ks[17], (DEPTH, D_MODEL, 2 * D_FF), D_MODEL ** -0.5),
        "conv_f_w": nrm(ks[18], (DEPTH, FFN_CONV_K, 2 * D_FF), FFN_CONV_K ** -0.5),
        "conv_f_b": nrm(ks[19], (DEPTH, 2 * D_FF), 0.02),
        "w_down": nrm(ks[20], (DEPTH, D_FF, D_MODEL), D_FF ** -0.5),
        "norm_f_g": 1.0 + nrm(ks[21], (D_MODEL,), 0.02),
    }


def reference(x_prompt, x_sample, cache_k_win, cache_v_win, state_conv_a, state_conv_ffn,
              norm1_g, w_in, conv_a_w, conv_a_b, ln_a_g, ln_a_b, w_a_out, attn_sinks, w_attn_out,
              w_o, norm2_g, w_up, conv_f_w, conv_f_b, w_down, norm_f_g):
    Bp, Lp = x_prompt.shape[0], x_prompt.shape[1]
    Ls = x_sample.shape[1]
    pos_p = jnp.arange(Lp, dtype=jnp.int32)
    pos_s = PAST_LEN + jnp.arange(Ls, dtype=jnp.int32)
    hp, hs = x_prompt, x_sample
    kp_l, vp_l, cap_l, cfp_l = [], [], [], []
    ks_l, vs_l, cas_l, cfs_l = [], [], [], []
    for i in range(DEPTH):
        lw = (norm1_g[i], w_in[i], conv_a_w[i], conv_a_b[i], ln_a_g[i], ln_a_b[i], w_a_out[i],
              w_attn_out[i], w_o[i], norm2_g[i], w_up[i], conv_f_w[i], conv_f_b[i], w_down[i])
        hp, kp, vp, cap, cfp = trunk_layer(
            hp, pos_p, functools.partial(swa_prompt, sinks=attn_sinks[i]),
            jnp.zeros((Bp, CONV_K - 1, CONV_WIDTH), hp.dtype),
            jnp.zeros((Bp, FFN_CONV_K - 1, 2 * D_FF), hp.dtype), *lw)
        hs, ks_, vs_, cas, cfs = trunk_layer(
            hs, pos_s, functools.partial(swa_sample, cache_k=cache_k_win[i], cache_v=cache_v_win[i],
                                         sinks=attn_sinks[i]),
            state_conv_a[i], state_conv_ffn[i], *lw)
        kp_l.append(kp); vp_l.append(vp); cap_l.append(cap); cfp_l.append(cfp)
        ks_l.append(ks_); vs_l.append(vs_); cas_l.append(cas); cfs_l.append(cfs)
    y_prompt = rms_norm(hp, norm_f_g)
    y_sample = rms_norm(hs, norm_f_g)
    return (y_prompt, y_sample,
            jnp.stack(kp_l), jnp.stack(vp_l), jnp.stack(cap_l), jnp.stack(cfp_l),
            jnp.stack(ks_l), jnp.stack(vs_l), jnp.stack(cas_l), jnp.stack(cfs_l))
```

```python
import functools

import jax
import jax.numpy as jnp
from jax import lax
from jax.experimental import pallas as pl
from jax.experimental.pallas import tpu as pltpu

F32 = jnp.float32
BF16 = jnp.bfloat16

D_MODEL = 4096
CONV_WIDTH = D_MODEL // 2
CONV_K = 31
HEAD_DIM = 128
N_HEADS = (D_MODEL // 2) // HEAD_DIM
N_KV_HEADS = N_HEADS // 4
GROUP = N_HEADS // N_KV_HEADS
ROT_DIM = HEAD_DIM // 4
ROPE_THETA = 500000.0
WINDOW = 128
BLOCK = 128
D_FF = 11008
FFN_CONV_K = 3
EPS = 1e-6
PAST_LEN = 8192
Q_WIDTH = N_HEADS * HEAD_DIM
KV_WIDTH = N_KV_HEADS * HEAD_DIM
A_END = 2 * CONV_WIDTH
V_END = A_END + Q_WIDTH + 2 * KV_WIDTH
QKV_WIDTH = Q_WIDTH + 2 * KV_WIDTH
SCALE = HEAD_DIM ** -0.5

V7X_VMEM_BYTES = 64 * 1024 * 1024
VMEM_LIMIT = V7X_VMEM_BYTES - 8 * 1024 * 1024
SUBLANES = 8
CONV_HALO = 32


def _params(n_axes):
    return pltpu.CompilerParams(dimension_semantics=("arbitrary",) * n_axes,
                                vmem_limit_bytes=VMEM_LIMIT)


def _sigmoid(x):
    return 1.0 / (1.0 + jnp.exp(-x))


def _rmsnorm_body(x_ref, g_ref, o_ref):
    x = x_ref[...]
    ms = jnp.mean(x * x, axis=-1, keepdims=True)
    o_ref[...] = (x * lax.rsqrt(ms + EPS) * g_ref[...]).astype(o_ref.dtype)


def _rmsnorm(x, g, out_dtype, tr):
    m, d = x.shape
    return pl.pallas_call(
        _rmsnorm_body,
        grid=(m // tr,),
        in_specs=[pl.BlockSpec((tr, d), lambda i: (i, 0)),
                  pl.BlockSpec((1, d), lambda i: (0, 0))],
        out_specs=pl.BlockSpec((tr, d), lambda i: (i, 0)),
        out_shape=jax.ShapeDtypeStruct((m, d), out_dtype),
        compiler_params=_params(1),
        name="rmsnorm",
    )(x, g.reshape(1, d))


def _rope_table_body(inv_ref, sign_ref, cos_ref, sin_ref, *, pos0, pos_step):
    rows = cos_ref.shape[0]
    i = pl.program_id(0)
    r = lax.broadcasted_iota(jnp.int32, (rows, HEAD_DIM), 0) + i * rows
    lane = lax.broadcasted_iota(jnp.int32, (rows, HEAD_DIM), 1)
    pos = (pos0 + pos_step * r).astype(F32)
    ang = pos * inv_ref[...]
    cos_ref[...] = jnp.where(lane < ROT_DIM, jnp.cos(ang), 1.0)
    sin_ref[...] = sign_ref[...] * jnp.sin(ang)


def _rope_tables(n_rows, pos0, pos_step):
    half = ROT_DIM // 2
    inv = ROPE_THETA ** (-2.0 * jnp.arange(half, dtype=F32) / ROT_DIM)
    zeros = jnp.zeros((HEAD_DIM - ROT_DIM,), F32)
    inv_full = jnp.concatenate([inv, inv, zeros]).reshape(1, HEAD_DIM)
    sign = jnp.concatenate([-jnp.ones((half,), F32), jnp.ones((half,), F32), zeros]).reshape(1, HEAD_DIM)
    tr = min(n_rows, 512)
    spec_c = pl.BlockSpec((1, HEAD_DIM), lambda i: (0, 0))
    spec_t = pl.BlockSpec((tr, HEAD_DIM), lambda i: (i, 0))
    return pl.pallas_call(
        functools.partial(_rope_table_body, pos0=pos0, pos_step=pos_step),
        grid=(n_rows // tr,),
        in_specs=[spec_c, spec_c],
        out_specs=[spec_t, spec_t],
        out_shape=[jax.ShapeDtypeStruct((n_rows, HEAD_DIM), F32)] * 2,
        compiler_params=_params(1),
        name="rope_tables",
    )(inv_full, sign)


def _ws_body(*refs, dots, n_lhs, n_extra, n_out, epilogue):
    n_dots = len(dots)
    lhs_refs = refs[:n_lhs]
    w_refs = refs[n_lhs:n_lhs + n_dots]
    ex_refs = refs[n_lhs + n_dots:n_lhs + n_dots + n_extra]
    out_refs = refs[n_lhs + n_dots + n_extra:n_lhs + n_dots + n_extra + n_out]
    wbf_refs = refs[n_lhs + n_dots + n_extra + n_out:]
    j = pl.program_id(0)
    i = pl.program_id(1)

    @pl.when(i == 0)
    def _cast_weights():
        for w_ref, s_ref in zip(w_refs, wbf_refs):
            s_ref[...] = w_ref[...].astype(BF16)

    accs = [jnp.dot(lhs_refs[li][...], s_ref[...], preferred_element_type=F32)
            for li, s_ref in zip(dots, wbf_refs)]
    for o_ref, r in zip(out_refs, epilogue(accs, ex_refs, j, i)):
        o_ref[...] = r.astype(o_ref.dtype)


def _ws_matmul(lhs, dots, extras, epilogue, out_dtypes, *, n_cols, tm, tn, name):
    m = lhs[0].shape[0]
    in_specs = [pl.BlockSpec((tm, a.shape[1]), lambda j, i: (i, 0)) for a in lhs]
    for _, w, off in dots:
        in_specs.append(pl.BlockSpec((w.shape[0], tn), lambda j, i, ob=off // tn: (0, ob + j)))
    for _, shape, imap in extras:
        in_specs.append(pl.BlockSpec(shape, imap))
    body = functools.partial(_ws_body, dots=tuple(l for l, _, _ in dots), n_lhs=len(lhs),
                             n_extra=len(extras), n_out=len(out_dtypes), epilogue=epilogue)
    return pl.pallas_call(
        body,
        grid=(n_cols // tn, m // tm),
        in_specs=in_specs,
        out_specs=[pl.BlockSpec((tm, tn), lambda j, i: (i, j)) for _ in out_dtypes],
        out_shape=[jax.ShapeDtypeStruct((m, n_cols), dt) for dt in out_dtypes],
        scratch_shapes=[pltpu.VMEM((w.shape[0], tn), BF16) for _, w, _ in dots],
        compiler_params=_params(2),
        name=name,
    )(*lhs, *[w for _, w, _ in dots], *[a for a, _, _ in extras])


def _glu_epilogue(accs, ex_refs, j, i):
    return [accs[0] * _sigmoid(accs[1])]


def _rope_epilogue(accs, ex_refs, j, i, *, tn):
    cos_ref, sin_ref = ex_refs
    z = accs[0]
    rotated = j < (Q_WIDTH + KV_WIDTH) // tn
    cos_t = jnp.where(rotated, cos_ref[...], 1.0)
    sin_t = jnp.where(rotated, sin_ref[...], 0.0)
    lane = lax.broadcasted_iota(jnp.int32, cos_t.shape, 1)
    half = ROT_DIM // 2
    outs = []
    for c in range(tn // HEAD_DIM):
        x = z[:, c * HEAD_DIM:(c + 1) * HEAD_DIM]
        partner = jnp.where(lane < half, pltpu.roll(x, HEAD_DIM - half, 1), pltpu.roll(x, half, 1))
        outs.append(x * cos_t + partner * sin_t)
    return [jnp.concatenate(outs, axis=1)]


def _merge_epilogue(accs, ex_refs, j, i):
    gate_a, br_a, gate_b, br_b = accs
    return [_sigmoid(gate_a) * br_a + _sigmoid(gate_b) * br_b]


def _residual_epilogue(accs, ex_refs, j, i):
    return [ex_refs[0][...] + accs[0]]


def _ln_swish(y, g_ref, b_ref):
    mu = jnp.mean(y, axis=-1, keepdims=True)
    yc = y - mu
    var = jnp.mean(yc * yc, axis=-1, keepdims=True)
    t = yc * lax.rsqrt(var + EPS) * g_ref[...] + b_ref[...]
    return t * _sigmoid(t)


def _conv_a_prompt_body(cur_ref, prev_ref, w_ref, b_ref, g_ref, beta_ref, o_ref, xx_ref, y_ref, *, tl):
    first = pl.program_id(1) == 0
    xx_ref[0:CONV_HALO, :] = jnp.where(first, 0.0, prev_ref[...])
    xx_ref[CONV_HALO:, :] = cur_ref[...]
    rows, lanes = 32, 512
    base = CONV_HALO - (CONV_K - 1)
    for rc in range(tl // rows):
        for cc in range(CONV_WIDTH // lanes):
            cs = slice(cc * lanes, (cc + 1) * lanes)
            acc = jnp.zeros((rows, lanes), F32) + b_ref[:, cs]
            for k in range(CONV_K):
                r0 = rc * rows + base + k
                acc = acc + w_ref[k:k + 1, cs] * xx_ref[r0:r0 + rows, cs]
            y_ref[rc * rows:(rc + 1) * rows, cs] = acc
    o_ref[...] = _ln_swish(y_ref[...], g_ref, beta_ref).astype(o_ref.dtype)


def _conv_a_prompt(u, conv_w, conv_b, ln_g, ln_b, *, n_batch, seq, tl):
    c = CONV_WIDTH
    nl = seq // tl
    halo_per_tile = tl // CONV_HALO
    row = lambda a: a.reshape(1, c)
    vec = pl.BlockSpec((1, c), lambda b, l: (0, 0))
    return pl.pallas_call(
        functools.partial(_conv_a_prompt_body, tl=tl),
        grid=(n_batch, nl),
        in_specs=[pl.BlockSpec((tl, c), lambda b, l: (b * nl + l, 0)),
                  pl.BlockSpec((CONV_HALO, c),
                               lambda b, l: (jnp.maximum((b * nl + l) * halo_per_tile - 1, 0), 0)),
                  pl.BlockSpec((CONV_K, c), lambda b, l: (0, 0)),
                  vec, vec, vec],
        out_specs=pl.BlockSpec((tl, c), lambda b, l: (b * nl + l, 0)),
        out_shape=jax.ShapeDtypeStruct((n_batch * seq, c), BF16),
        scratch_shapes=[pltpu.VMEM((tl + CONV_HALO, c), F32), pltpu.VMEM((tl, c), F32)],
        compiler_params=_params(2),
        name="conv_a_prompt",
    )(u, u, conv_w, row(conv_b), row(ln_g), row(ln_b))


def _conv_a_sample_body(hist_ref, u_ref, w_ref, b_ref, g_ref, beta_ref, o_ref):
    w_hist = w_ref[0:CONV_K - 1, :]
    y = jnp.sum(hist_ref[...] * w_hist[None], axis=1)
    y = y + u_ref[...] * w_ref[CONV_K - 1:CONV_K, :] + b_ref[...]
    o_ref[...] = _ln_swish(y, g_ref, beta_ref).astype(o_ref.dtype)


def _conv_a_sample(hist, u, conv_w, conv_b, ln_g, ln_b, *, bs):
    n, c = u.shape
    row = lambda a: a.reshape(1, c)
    vec = pl.BlockSpec((1, c), lambda b: (0, 0))
    return pl.pallas_call(
        _conv_a_sample_body,
        grid=(n // bs,),
        in_specs=[pl.BlockSpec((bs, CONV_K - 1, c), lambda b: (b, 0, 0)),
                  pl.BlockSpec((bs, c), lambda b: (b, 0)),
                  pl.BlockSpec((CONV_K, c), lambda b: (0, 0)),
                  vec, vec, vec],
        out_specs=pl.BlockSpec((bs, c), lambda b: (b, 0)),
        out_shape=jax.ShapeDtypeStruct((n, c), BF16),
        compiler_params=_params(1),
        name="conv_a_sample",
    )(hist, u, conv_w, row(conv_b), row(ln_g), row(ln_b))


def _attn_prompt_body(q_ref, kc_ref, kp_ref, vc_ref, vp_ref, sink_ref, o_ref):
    n = pl.program_id(1)
    rows = GROUP * BLOCK
    r = lax.broadcasted_iota(jnp.int32, (rows, 2 * BLOCK), 0) % BLOCK
    c = lax.broadcasted_iota(jnp.int32, (rows, 2 * BLOCK), 1)
    diff = (r + BLOCK) - c
    mask = (diff >= 0) & (diff < WINDOW) & ((c >= BLOCK) | (n > 0))
    for kh in range(N_KV_HEADS):
        ks = slice(kh * HEAD_DIM, (kh + 1) * HEAD_DIM)
        kk = jnp.concatenate([kp_ref[:, ks], kc_ref[:, ks]], axis=0).astype(BF16)
        vv = jnp.concatenate([vp_ref[:, ks], vc_ref[:, ks]], axis=0).astype(BF16)
        heads = [kh * GROUP + g for g in range(GROUP)]
        qg = jnp.concatenate([q_ref[:, h * HEAD_DIM:(h + 1) * HEAD_DIM] for h in heads],
                             axis=0).astype(BF16)
        sink = jnp.concatenate([jnp.broadcast_to(sink_ref[h:h + 1, 0:1], (BLOCK, 1)) for h in heads],
                               axis=0)
        s = lax.dot_general(qg, kk, (((1,), (1,)), ((), ())), preferred_element_type=F32) * SCALE
        s = jnp.where(mask, s, -jnp.inf)
        m = jnp.maximum(jnp.max(s, axis=-1, keepdims=True), sink)
        p = jnp.exp(s - m)
        den = jnp.sum(p, axis=-1, keepdims=True) + jnp.exp(sink - m)
        o = jnp.dot((p / den).astype(BF16), vv, preferred_element_type=F32)
        for g, h in enumerate(heads):
            o_ref[:, h * HEAD_DIM:(h + 1) * HEAD_DIM] = o[g * BLOCK:(g + 1) * BLOCK].astype(o_ref.dtype)


def _attn_prompt(qkv, sinks, *, n_batch, seq):
    nb = seq // BLOCK
    k_col = Q_WIDTH // KV_WIDTH
    cur = lambda col: (lambda b, n: (b * nb + n, col))
    prev = lambda col: (lambda b, n: (jnp.maximum(b * nb + n - 1, 0), col))
    kv = lambda imap: pl.BlockSpec((BLOCK, KV_WIDTH), imap)
    return pl.pallas_call(
        _attn_prompt_body,
        grid=(n_batch, nb),
        in_specs=[pl.BlockSpec((BLOCK, Q_WIDTH), cur(0)),
                  kv(cur(k_col)), kv(prev(k_col)), kv(cur(k_col + 1)), kv(prev(k_col + 1)),
                  pl.BlockSpec((N_HEADS, HEAD_DIM), lambda b, n: (0, 0))],
        out_specs=pl.BlockSpec((BLOCK, Q_WIDTH), cur(0)),
        out_shape=jax.ShapeDtypeStruct((n_batch * seq, Q_WIDTH), BF16),
        compiler_params=_params(2),
        name="attn_prompt",
    )(qkv, qkv, qkv, qkv, qkv, sinks)


def _attn_sample_body(qkv_ref, ck_ref, cv_ref, sink_ref, o_ref, *, bs):
    j = lax.broadcasted_iota(jnp.int32, (GROUP, WINDOW), 1)
    valid = j >= 1
    k_row = N_HEADS
    v_row = N_HEADS + N_KV_HEADS

    def per_sample(b, carry):
        for kh in range(N_KV_HEADS):
            hs = slice(kh * GROUP, (kh + 1) * GROUP)
            ks = slice(kh * HEAD_DIM, (kh + 1) * HEAD_DIM)
            qg = qkv_ref[b, hs, :].astype(BF16)
            kn = qkv_ref[b, k_row + kh:k_row + kh + 1, :].astype(BF16).astype(F32)
            vn = qkv_ref[b, v_row + kh:v_row + kh + 1, :].astype(BF16).astype(F32)
            kc = ck_ref[b, :, ks].astype(BF16)
            vc = cv_ref[b, :, ks].astype(BF16)
            sink = sink_ref[hs, 0:1]
            s = lax.dot_general(qg, kc, (((1,), (1,)), ((), ())), preferred_element_type=F32) * SCALE
            s = jnp.where(valid, s, -jnp.inf)
            sn = jnp.sum(qg.astype(F32) * kn, axis=-1, keepdims=True) * SCALE
            m = jnp.maximum(jnp.maximum(jnp.max(s, axis=-1, keepdims=True), sn), sink)
            p = jnp.exp(s - m)
            pn = jnp.exp(sn - m)
            den = jnp.sum(p, axis=-1, keepdims=True) + pn + jnp.exp(sink - m)
            o = jnp.dot((p / den).astype(BF16), vc, preferred_element_type=F32)
            o = o + (pn / den).astype(BF16).astype(F32) * vn
            o_ref[b, hs, :] = o.astype(o_ref.dtype)
        return carry

    lax.fori_loop(0, bs, per_sample, 0)


def _attn_sample(qkv, cache_k, cache_v, sinks, *, bs):
    n = qkv.shape[0]
    n_rows = QKV_WIDTH // HEAD_DIM
    cache = pl.BlockSpec((bs, WINDOW, KV_WIDTH), lambda b: (b, 0, 0))
    out = pl.pallas_call(
        functools.partial(_attn_sample_body, bs=bs),
        grid=(n // bs,),
        in_specs=[pl.BlockSpec((bs, n_rows, HEAD_DIM), lambda b: (b, 0, 0)), cache, cache,
                  pl.BlockSpec((N_HEADS, HEAD_DIM), lambda b: (0, 0))],
        out_specs=pl.BlockSpec((bs, N_HEADS, HEAD_DIM), lambda b: (b, 0, 0)),
        out_shape=jax.ShapeDtypeStruct((n, N_HEADS, HEAD_DIM), BF16),
        compiler_params=_params(1),
        name="attn_sample",
    )(qkv.reshape(n, n_rows, HEAD_DIM), cache_k, cache_v, sinks)
    return out.reshape(n, Q_WIDTH)


def _ffn_cast(i, w_refs, wbf_refs):
    @pl.when(i == 0)
    def _cast_weights():
        for w_ref, s_ref in zip(w_refs, wbf_refs):
            s_ref[...] = w_ref[...].astype(BF16)


def _ffn_prompt_body(x_ref, wg_ref, wv_ref, cwg_ref, cwv_ref, cbg_ref, cbv_ref,
                     f_ref, tg_ref, tv_ref, wg_bf, wv_bf, eg_ref, ev_ref, *, tm, tiles_per_seq):
    i = pl.program_id(1)
    _ffn_cast(i, (wg_ref, wv_ref), (wg_bf, wv_bf))

    @pl.when(i % tiles_per_seq == 0)
    def _zero_history():
        eg_ref[0:SUBLANES, :] = jnp.zeros((SUBLANES, eg_ref.shape[1]), F32)
        ev_ref[0:SUBLANES, :] = jnp.zeros((SUBLANES, ev_ref.shape[1]), F32)

    x = x_ref[...]
    ys = []
    for w_bf, e_ref, cw_ref, cb_ref, t_ref in ((wg_bf, eg_ref, cwg_ref, cbg_ref, tg_ref),
                                               (wv_bf, ev_ref, cwv_ref, cbv_ref, tv_ref)):
        e_ref[SUBLANES:, :] = jnp.dot(x, w_bf[...], preferred_element_type=F32)
        y = cb_ref[...]
        for k in range(FFN_CONV_K):
            r0 = SUBLANES - (FFN_CONV_K - 1) + k
            y = y + cw_ref[k:k + 1, :] * e_ref[r0:r0 + tm, :]
        ys.append(y)
        tail = e_ref[tm:tm + SUBLANES, :]
        t_ref[0] = tail
        e_ref[0:SUBLANES, :] = tail
    f_ref[...] = (ys[0] * _sigmoid(ys[0]) * ys[1]).astype(f_ref.dtype)


def _ffn_sample_body(x_ref, wg_ref, wv_ref, cwg_ref, cwv_ref, cbg_ref, cbv_ref,
                     h0g_ref, h0v_ref, h1g_ref, h1v_ref,
                     f_ref, rg_ref, rv_ref, wg_bf, wv_bf):
    _ffn_cast(pl.program_id(1), (wg_ref, wv_ref), (wg_bf, wv_bf))
    x = x_ref[...]
    ys = []
    for w_bf, cw_ref, cb_ref, h0_ref, h1_ref, r_ref in ((wg_bf, cwg_ref, cbg_ref, h0g_ref, h1g_ref, rg_ref),
                                                        (wv_bf, cwv_ref, cbv_ref, h0v_ref, h1v_ref, rv_ref)):
        d = jnp.dot(x, w_bf[...], preferred_element_type=F32)
        r_ref[...] = d
        ys.append(cw_ref[0:1, :] * h0_ref[...] + cw_ref[1:2, :] * h1_ref[...] + cw_ref[2:3, :] * d
                  + cb_ref[...])
    f_ref[...] = (ys[0] * _sigmoid(ys[0]) * ys[1]).astype(f_ref.dtype)


def _ffn_up(x, w_up, conv_w, conv_b, *, tm, tn, seq=None, hist=None):
    m, d = x.shape
    half_blocks = D_FF // tn
    col = lambda off: (lambda j, i: (0, off + j))
    conv_b = conv_b.reshape(1, 2 * D_FF)
    in_specs = [pl.BlockSpec((tm, d), lambda j, i: (i, 0)),
                pl.BlockSpec((d, tn), col(0)), pl.BlockSpec((d, tn), col(half_blocks)),
                pl.BlockSpec((FFN_CONV_K, tn), col(0)), pl.BlockSpec((FFN_CONV_K, tn), col(half_blocks)),
                pl.BlockSpec((1, tn), col(0)), pl.BlockSpec((1, tn), col(half_blocks))]
    args = [x, w_up, w_up, conv_w, conv_w, conv_b, conv_b]
    f_spec = pl.BlockSpec((tm, tn), lambda j, i: (i, j))
    f_shape = jax.ShapeDtypeStruct((m, D_FF), BF16)
    scratch = [pltpu.VMEM((d, tn), BF16)] * 2
    if hist is None:
        tiles_per_seq = seq // tm
        n_seq = m // seq
        body = functools.partial(_ffn_prompt_body, tm=tm, tiles_per_seq=tiles_per_seq)
        raw_spec = pl.BlockSpec((1, SUBLANES, tn), lambda j, i: (i // tiles_per_seq, 0, j))
        raw_shape = jax.ShapeDtypeStruct((n_seq, SUBLANES, D_FF), F32)
        scratch = scratch + [pltpu.VMEM((tm + SUBLANES, tn), F32)] * 2
    else:
        body = _ffn_sample_body
        row = lambda off: (lambda j, i: (i, off + j))
        in_specs += [pl.BlockSpec((tm, tn), row(k * half_blocks)) for k in range(4)]
        args += [hist] * 4
        raw_spec = f_spec
        raw_shape = jax.ShapeDtypeStruct((m, D_FF), F32)
    return pl.pallas_call(
        body,
        grid=(half_blocks, m // tm),
        in_specs=in_specs,
        out_specs=[f_spec, raw_spec, raw_spec],
        out_shape=[f_shape, raw_shape, raw_shape],
        scratch_shapes=scratch,
        compiler_params=_params(2),
        name="ffn_up_prompt" if hist is None else "ffn_up_sample",
    )(*args)


def _trunk_rows(x, rope, conv_a, attend, ffn, weights, *, tm, tm_merge, tm_down, tr):
    (norm1_g, w_in, w_a_out, w_attn_out, w_o, norm2_g, w_down, norm_f_g) = weights
    cos_t, sin_t, rope_map = rope
    rows = x.shape[0]
    tile = lambda j, i: (i, j)
    xn = _rmsnorm(x, norm1_g, BF16, tr)
    (u,) = _ws_matmul([xn], [(0, w_in, 0), (0, w_in, CONV_WIDTH)], [], _glu_epilogue, [F32],
                      n_cols=CONV_WIDTH, tm=tm, tn=256, name="in_glu")
    tn_qkv = KV_WIDTH
    (qkv,) = _ws_matmul([xn], [(0, w_in, A_END)],
                        [(cos_t, (tm, HEAD_DIM), rope_map), (sin_t, (tm, HEAD_DIM), rope_map)],
                        functools.partial(_rope_epilogue, tn=tn_qkv), [F32],
                        n_cols=QKV_WIDTH, tm=tm, tn=tn_qkv, name="in_qkv_rope")
    c = conv_a(u)
    o = attend(qkv)
    (merged,) = _ws_matmul([xn, c, o],
                           [(0, w_in, V_END), (1, w_a_out, 0), (0, w_in, V_END + D_MODEL), (2, w_attn_out, 0)],
                           [], _merge_epilogue, [BF16],
                           n_cols=D_MODEL, tm=tm_merge, tn=256, name="gated_merge")
    (h,) = _ws_matmul([merged], [(0, w_o, 0)], [(x, (tm, 512), tile)], _residual_epilogue, [F32],
                      n_cols=D_MODEL, tm=tm, tn=512, name="out_proj")
    hn = _rmsnorm(h, norm2_g, BF16, tr)
    f, raw_g, raw_v = ffn(hn)
    (y,) = _ws_matmul([f], [(0, w_down, 0)], [(h, (tm_down, 256), tile)], _residual_epilogue, [F32],
                      n_cols=D_MODEL, tm=tm_down, tn=256, name="ffn_down")
    return _rmsnorm(y, norm_f_g, F32, tr), u, qkv, raw_g, raw_v


def kernel(x_prompt, x_sample, cache_k_win, cache_v_win, state_conv_a, state_conv_ffn, norm1_g, w_in,
           conv_a_w, conv_a_b, ln_a_g, ln_a_b, w_a_out, attn_sinks, w_attn_out, w_o, norm2_g, w_up,
           conv_f_w, conv_f_b, w_down, norm_f_g):
    n_batch, seq, d = x_prompt.shape
    n_dec = x_sample.shape[0]
    assert x_sample.shape[1] == 1 and w_in.shape[0] == 1
    weights = (norm1_g[0], w_in[0], w_a_out[0], w_attn_out[0], w_o[0], norm2_g[0], w_down[0], norm_f_g)
    sinks = jnp.broadcast_to(attn_sinks[0][:, None], (N_HEADS, HEAD_DIM))
    conv_a_args = (conv_a_w[0], conv_a_b[0], ln_a_g[0], ln_a_b[0])
    ffn_args = (w_up[0], conv_f_w[0], conv_f_b[0])

    tm = 1024
    cos_p, sin_p = _rope_tables(seq, 0, 1)
    y_p, u_p, qkv_p, raw_g, raw_v = _trunk_rows(
        x_prompt.reshape(n_batch * seq, d),
        (cos_p, sin_p, lambda j, i: (i % (seq // tm), 0)),
        lambda u: _conv_a_prompt(u, *conv_a_args, n_batch=n_batch, seq=seq, tl=128),
        lambda qkv: _attn_prompt(qkv, sinks, n_batch=n_batch, seq=seq),
        lambda hn: _ffn_up(hn, *ffn_args, tm=tm, tn=256, seq=seq),
        weights, tm=tm, tm_merge=512, tm_down=256, tr=256)
    y_prompt = y_p.reshape(n_batch, seq, d)
    qkv_p = qkv_p.reshape(n_batch, seq, QKV_WIDTH)
    k_win_p = qkv_p[:, seq - WINDOW:, Q_WIDTH:Q_WIDTH + KV_WIDTH].reshape(1, n_batch, WINDOW, N_KV_HEADS, HEAD_DIM)
    v_win_p = qkv_p[:, seq - WINDOW:, Q_WIDTH + KV_WIDTH:].reshape(1, n_batch, WINDOW, N_KV_HEADS, HEAD_DIM)
    conv_a_p = u_p.reshape(n_batch, seq, CONV_WIDTH)[None, :, seq - (CONV_K - 1):]
    keep = SUBLANES - (FFN_CONV_K - 1)
    conv_ffn_p = jnp.concatenate([raw_g[:, keep:], raw_v[:, keep:]], axis=-1)[None]

    cos_s, sin_s = _rope_tables(n_dec, PAST_LEN, 0)
    cache_k = cache_k_win[0].reshape(n_dec, WINDOW, KV_WIDTH)
    cache_v = cache_v_win[0].reshape(n_dec, WINDOW, KV_WIDTH)
    ffn_hist = state_conv_ffn[0].reshape(n_dec, (FFN_CONV_K - 1) * 2 * D_FF)
    y_s, u_s, qkv_s, raw_g_s, raw_v_s = _trunk_rows(
        x_sample.reshape(n_dec, d),
        (cos_s, sin_s, lambda j, i: (0, 0)),
        lambda u: _conv_a_sample(state_conv_a[0], u, *conv_a_args, bs=8),
        lambda qkv: _attn_sample(qkv, cache_k, cache_v, sinks, bs=8),
        lambda hn: _ffn_up(hn, *ffn_args, tm=n_dec, tn=256, hist=ffn_hist),
        weights, tm=n_dec, tm_merge=n_dec, tm_down=n_dec, tr=n_dec)
    y_sample = y_s.reshape(n_dec, 1, d)
    k_new = qkv_s[:, None, Q_WIDTH:Q_WIDTH + KV_WIDTH]
    v_new = qkv_s[:, None, Q_WIDTH + KV_WIDTH:]
    k_win_s = jnp.concatenate([cache_k[:, 1:], k_new], axis=1).reshape(1, n_dec, WINDOW, N_KV_HEADS, HEAD_DIM)
    v_win_s = jnp.concatenate([cache_v[:, 1:], v_new], axis=1).reshape(1, n_dec, WINDOW, N_KV_HEADS, HEAD_DIM)
    conv_a_s = jnp.concatenate([state_conv_a[0][:, 1:], u_s[:, None]], axis=1)[None]
    up_new = jnp.concatenate([raw_g_s, raw_v_s], axis=-1)[:, None]
    conv_ffn_s = jnp.concatenate([state_conv_ffn[0][:, 1:], up_new], axis=1)[None]

    return (y_prompt, y_sample, k_win_p, v_win_p, conv_a_p, conv_ffn_p,
            k_win_s, v_win_s, conv_a_s, conv_ffn_s)
```

```python
import functools

import jax
import jax.numpy as jnp
from jax import lax
from jax.experimental import pallas as pl
from jax.experimental.pallas import tpu as pltpu

F32 = jnp.float32
BF16 = jnp.bfloat16

D_MODEL = 4096
CONV_WIDTH = D_MODEL // 2
CONV_K = 31
HEAD_DIM = 128
N_HEADS = (D_MODEL // 2) // HEAD_DIM
N_KV_HEADS = N_HEADS // 4
GROUP = N_HEADS // N_KV_HEADS
ROT_DIM = HEAD_DIM // 4
ROPE_THETA = 500000.0
WINDOW = 128
BLOCK = 128
D_FF = 11008
FFN_CONV_K = 3
EPS = 1e-6
PAST_LEN = 8192
Q_WIDTH = N_HEADS * HEAD_DIM
KV_WIDTH = N_KV_HEADS * HEAD_DIM
A_END = 2 * CONV_WIDTH
V_END = A_END + Q_WIDTH + 2 * KV_WIDTH
QKV_WIDTH = Q_WIDTH + 2 * KV_WIDTH
SCALE = HEAD_DIM ** -0.5

V7X_VMEM_BYTES = 64 * 1024 * 1024
VMEM_LIMIT = V7X_VMEM_BYTES - 8 * 1024 * 1024
SUBLANES = 8
CONV_HALO = 32
ROW_CHUNK = 256


def _params(n_axes):
    return pltpu.CompilerParams(dimension_semantics=("arbitrary",) * n_axes,
                                vmem_limit_bytes=VMEM_LIMIT)


def _sigmoid(x):
    return 1.0 / (1.0 + jnp.exp(-x))


def _row_chunks(tm):
    rc = min(tm, ROW_CHUNK)
    return [slice(r * rc, (r + 1) * rc) for r in range(tm // rc)]


def _rmsnorm_body(x_ref, g_ref, o_ref):
    x = x_ref[...]
    ms = jnp.mean(x * x, axis=-1, keepdims=True)
    o_ref[...] = (x * lax.rsqrt(ms + EPS) * g_ref[...]).astype(o_ref.dtype)


def _rmsnorm(x, g, out_dtype, tr):
    m, d = x.shape
    tr = min(tr, m)
    return pl.pallas_call(
        _rmsnorm_body,
        grid=(m // tr,),
        in_specs=[pl.BlockSpec((tr, d), lambda i: (i, 0)),
                  pl.BlockSpec((1, d), lambda i: (0, 0))],
        out_specs=pl.BlockSpec((tr, d), lambda i: (i, 0)),
        out_shape=jax.ShapeDtypeStruct((m, d), out_dtype),
        compiler_params=_params(1),
        name="rmsnorm",
    )(x, g.reshape(1, d))


def _rope_table_body(inv_ref, sign_ref, cos_ref, sin_ref, *, pos0, pos_step):
    rows = cos_ref.shape[0]
    i = pl.program_id(0)
    r = lax.broadcasted_iota(jnp.int32, (rows, HEAD_DIM), 0) + i * rows
    lane = lax.broadcasted_iota(jnp.int32, (rows, HEAD_DIM), 1)
    pos = (pos0 + pos_step * r).astype(F32)
    ang = pos * inv_ref[...]
    cos_ref[...] = jnp.where(lane < ROT_DIM, jnp.cos(ang), 1.0)
    sin_ref[...] = sign_ref[...] * jnp.sin(ang)


def _rope_tables(n_rows, pos0, pos_step):
    half = ROT_DIM // 2
    inv = ROPE_THETA ** (-2.0 * jnp.arange(half, dtype=F32) / ROT_DIM)
    zeros = jnp.zeros((HEAD_DIM - ROT_DIM,), F32)
    inv_full = jnp.concatenate([inv, inv, zeros]).reshape(1, HEAD_DIM)
    sign = jnp.concatenate([-jnp.ones((half,), F32), jnp.ones((half,), F32), zeros]).reshape(1, HEAD_DIM)
    tr = min(n_rows, 512)
    spec_c = pl.BlockSpec((1, HEAD_DIM), lambda i: (0, 0))
    spec_t = pl.BlockSpec((tr, HEAD_DIM), lambda i: (i, 0))
    return pl.pallas_call(
        functools.partial(_rope_table_body, pos0=pos0, pos_step=pos_step),
        grid=(n_rows // tr,),
        in_specs=[spec_c, spec_c],
        out_specs=[spec_t, spec_t],
        out_shape=[jax.ShapeDtypeStruct((n_rows, HEAD_DIM), F32)] * 2,
        compiler_params=_params(1),
        name="rope_tables",
    )(inv_full, sign)


def _cast_weights(w_refs, wbf_refs):
    for w_ref, s_ref in zip(w_refs, wbf_refs):
        s_ref[...] = w_ref[...].astype(BF16)


def _ws_body(*refs, dot_lhs, n_lhs, n_extra, n_out, tms, epilogue):
    n_dots = len(dot_lhs)
    pos = 0
    def take(n):
        nonlocal pos
        out = refs[pos:pos + n]
        pos += n
        return out
    lhs_p, lhs_s = take(n_lhs), take(n_lhs)
    w_refs = take(n_dots)
    ex_p, ex_s = take(n_extra), take(n_extra)
    out_p, out_s = take(n_out), take(n_out)
    wbf_refs = take(n_dots)
    j = pl.program_id(0)
    i = pl.program_id(1)

    def run_rows(lhs_refs, ex_refs, out_refs, tm):
        for rows in _row_chunks(tm):
            accs = [jnp.dot(lhs_refs[li][rows, :], s_ref[...], preferred_element_type=F32)
                    for li, s_ref in zip(dot_lhs, wbf_refs)]
            for o_ref, val in zip(out_refs, epilogue(accs, ex_refs, rows, j)):
                o_ref[rows, :] = val.astype(o_ref.dtype)

    @pl.when(i == 0)
    def _new_column_block():
        _cast_weights(w_refs, wbf_refs)
        run_rows(lhs_s, ex_s, out_s, tms[1])

    run_rows(lhs_p, ex_p, out_p, tms[0])


def _ws_matmul(lhs, dots, extras, epilogue, out_dtypes, *, n_cols, tm, tn, name):
    m_p = lhs[0][0][0].shape[0]
    m_s = lhs[0][0][1].shape[0]
    in_specs, args = [], []
    for grp in (0, 1):
        for arrs, kb, ks in lhs:
            if grp == 0:
                in_specs.append(pl.BlockSpec((tm, ks), lambda j, i, kb=kb: (i, kb)))
            else:
                in_specs.append(pl.BlockSpec((m_s, ks), lambda j, i, kb=kb: (0, kb)))
            args.append(arrs[grp])
    for li, w, off in dots:
        _, kb, ks = lhs[li]
        in_specs.append(pl.BlockSpec((ks, tn), lambda j, i, kb=kb, ob=off // tn: (kb, ob + j)))
        args.append(w)
    for grp in (0, 1):
        for arrs, cols, imap in extras:
            if grp == 0:
                in_specs.append(pl.BlockSpec((tm, cols), imap))
            else:
                in_specs.append(pl.BlockSpec((m_s, cols), (lambda j, i: (0, j)) if cols == tn else (lambda j, i: (0, 0))))
            args.append(arrs[grp])
    out_specs = ([pl.BlockSpec((tm, tn), lambda j, i: (i, j)) for _ in out_dtypes]
                 + [pl.BlockSpec((m_s, tn), lambda j, i: (0, j)) for _ in out_dtypes])
    out_shape = ([jax.ShapeDtypeStruct((m_p, n_cols), dt) for dt in out_dtypes]
                 + [jax.ShapeDtypeStruct((m_s, n_cols), dt) for dt in out_dtypes])
    body = functools.partial(_ws_body, dot_lhs=tuple(l for l, _, _ in dots), n_lhs=len(lhs),
                             n_extra=len(extras), n_out=len(out_dtypes), tms=(tm, m_s), epilogue=epilogue)
    outs = pl.pallas_call(
        body,
        grid=(n_cols // tn, m_p // tm),
        in_specs=in_specs,
        out_specs=out_specs,
        out_shape=out_shape,
        scratch_shapes=[pltpu.VMEM((lhs[li][2], tn), BF16) for li, _, _ in dots],
        compiler_params=_params(2),
        name=name,
    )(*args)
    n = len(out_dtypes)
    return list(zip(outs[:n], outs[n:]))


def _glu_epilogue(accs, ex_refs, rows, j):
    return [accs[0] * _sigmoid(accs[1])]


def _rope_epilogue(accs, ex_refs, rows, j, *, tn):
    cos_ref, sin_ref = ex_refs
    z = accs[0]
    rotated = j < (Q_WIDTH + KV_WIDTH) // tn
    cos_t = jnp.where(rotated, cos_ref[rows, :], 1.0)
    sin_t = jnp.where(rotated, sin_ref[rows, :], 0.0)
    lane = lax.broadcasted_iota(jnp.int32, cos_t.shape, 1)
    half = ROT_DIM // 2
    outs = []
    for c in range(tn // HEAD_DIM):
        x = z[:, c * HEAD_DIM:(c + 1) * HEAD_DIM]
        partner = jnp.where(lane < half, pltpu.roll(x, HEAD_DIM - half, 1), pltpu.roll(x, half, 1))
        outs.append(x * cos_t + partner * sin_t)
    return [jnp.concatenate(outs, axis=1)]


def _merge_epilogue(accs, ex_refs, rows, j):
    gate_a, br_a, gate_b, br_b = accs
    return [_sigmoid(gate_a) * br_a + _sigmoid(gate_b) * br_b]


def _residual_epilogue(accs, ex_refs, rows, j):
    return [ex_refs[0][rows, :] + accs[0]]


def _ln_swish(y, g_ref, b_ref):
    mu = jnp.mean(y, axis=-1, keepdims=True)
    yc = y - mu
    var = jnp.mean(yc * yc, axis=-1, keepdims=True)
    t = yc * lax.rsqrt(var + EPS) * g_ref[...] + b_ref[...]
    return t * _sigmoid(t)


def _conv_a_prompt_body(cur_ref, prev_ref, w_ref, b_ref, g_ref, beta_ref, o_ref, xx_ref, sh_ref, y_ref, *, tl):
    first = pl.program_id(1) == 0
    xx_ref[0:CONV_HALO, :] = jnp.where(first, 0.0, prev_ref[...])
    xx_ref[CONV_HALO:, :] = cur_ref[...]
    n_sh = sh_ref.shape[1]
    for b in range(1, SUBLANES):
        sh_ref[b - 1] = xx_ref[b:b + n_sh, :]
    rows, lanes = 32, 512
    base = CONV_HALO - (CONV_K - 1)

    def row_chunk(rc, carry):
        r0 = pl.multiple_of(rc * rows, rows)
        for cc in range(CONV_WIDTH // lanes):
            cs = slice(cc * lanes, (cc + 1) * lanes)
            acc = jnp.zeros((rows, lanes), F32) + b_ref[:, cs]
            for k in range(CONV_K):
                a, b = divmod(base + k, SUBLANES)
                start = r0 + a * SUBLANES
                if b == 0:
                    x = xx_ref[pl.ds(start, rows), cs]
                else:
                    x = sh_ref[b - 1, pl.ds(start, rows), cs]
                acc = acc + w_ref[k:k + 1, cs] * x
            y_ref[pl.ds(r0, rows), cs] = acc
        return carry

    lax.fori_loop(0, tl // rows, row_chunk, 0)
    o_ref[...] = _ln_swish(y_ref[...], g_ref, beta_ref).astype(o_ref.dtype)


def _conv_a_prompt(u, conv_w, conv_b, ln_g, ln_b, *, n_batch, seq, tl):
    c = CONV_WIDTH
    nl = seq // tl
    halo_per_tile = tl // CONV_HALO
    row = lambda a: a.reshape(1, c)
    vec = pl.BlockSpec((1, c), lambda b, l: (0, 0))
    return pl.pallas_call(
        functools.partial(_conv_a_prompt_body, tl=tl),
        grid=(n_batch, nl),
        in_specs=[pl.BlockSpec((tl, c), lambda b, l: (b * nl + l, 0)),
                  pl.BlockSpec((CONV_HALO, c),
                               lambda b, l: (jnp.maximum((b * nl + l) * halo_per_tile - 1, 0), 0)),
                  pl.BlockSpec((CONV_K, c), lambda b, l: (0, 0)),
                  vec, vec, vec],
        out_specs=pl.BlockSpec((tl, c), lambda b, l: (b * nl + l, 0)),
        out_shape=jax.ShapeDtypeStruct((n_batch * seq, c), BF16),
        scratch_shapes=[pltpu.VMEM((tl + CONV_HALO, c), F32),
                        pltpu.VMEM((SUBLANES - 1, tl + CONV_HALO - SUBLANES, c), F32),
                        pltpu.VMEM((tl, c), F32)],
        compiler_params=_params(2),
        name="conv_a_prompt",
    )(u, u, conv_w, row(conv_b), row(ln_g), row(ln_b))


def _conv_a_sample_body(hist_ref, u_ref, w_ref, b_ref, g_ref, beta_ref, o_ref, new_ref):
    u = u_ref[...]
    y = u * w_ref[CONV_K - 1:CONV_K, :] + b_ref[...]
    for k in range(CONV_K - 1):
        y = y + hist_ref[k] * w_ref[k:k + 1, :]
    o_ref[...] = _ln_swish(y, g_ref, beta_ref).astype(o_ref.dtype)
    new_ref[0:CONV_K - 2] = hist_ref[1:CONV_K - 1]
    new_ref[CONV_K - 2] = u


def _conv_a_sample(hist_t, u, conv_w, conv_b, ln_g, ln_b, *, bs):
    n, c = u.shape
    row = lambda a: a.reshape(1, c)
    vec = pl.BlockSpec((1, c), lambda b: (0, 0))
    hist_spec = pl.BlockSpec((CONV_K - 1, bs, c), lambda b: (0, b, 0))
    return pl.pallas_call(
        _conv_a_sample_body,
        grid=(n // bs,),
        in_specs=[hist_spec,
                  pl.BlockSpec((bs, c), lambda b: (b, 0)),
                  pl.BlockSpec((CONV_K, c), lambda b: (0, 0)),
                  vec, vec, vec],
        out_specs=[pl.BlockSpec((bs, c), lambda b: (b, 0)), hist_spec],
        out_shape=[jax.ShapeDtypeStruct((n, c), BF16), jax.ShapeDtypeStruct(hist_t.shape, F32)],
        compiler_params=_params(1),
        name="conv_a_sample",
    )(hist_t, u, conv_w, row(conv_b), row(ln_g), row(ln_b))


def _softmax_with_sink(s, sink):
    m = jnp.maximum(jnp.max(s, axis=-1, keepdims=True), sink)
    p = jnp.exp(s - m)
    return p / (jnp.sum(p, axis=-1, keepdims=True) + jnp.exp(sink - m))


def _attn_prompt_body(q_ref, kc_ref, kp_ref, vc_ref, vp_ref, sink_ref, o_ref):
    n = pl.program_id(1)
    rows = GROUP * BLOCK
    r = lax.broadcasted_iota(jnp.int32, (rows, 2 * BLOCK), 0) % BLOCK
    c = lax.broadcasted_iota(jnp.int32, (rows, 2 * BLOCK), 1)
    diff = (r + BLOCK) - c
    mask = (diff >= 0) & (diff < WINDOW) & ((c >= BLOCK) | (n > 0))
    for kh in range(N_KV_HEADS):
        ks = slice(kh * HEAD_DIM, (kh + 1) * HEAD_DIM)
        kk = jnp.concatenate([kp_ref[:, ks], kc_ref[:, ks]], axis=0).astype(BF16)
        vv = jnp.concatenate([vp_ref[:, ks], vc_ref[:, ks]], axis=0).astype(BF16)
        heads = [kh * GROUP + g for g in range(GROUP)]
        qg = jnp.concatenate([q_ref[:, h * HEAD_DIM:(h + 1) * HEAD_DIM] for h in heads],
                             axis=0).astype(BF16)
        sink = jnp.concatenate([jnp.broadcast_to(sink_ref[h:h + 1, 0:1], (BLOCK, 1)) for h in heads],
                               axis=0)
        s = lax.dot_general(qg, kk, (((1,), (1,)), ((), ())), preferred_element_type=F32) * SCALE
        p = _softmax_with_sink(jnp.where(mask, s, -jnp.inf), sink)
        o = jnp.dot(p.astype(BF16), vv, preferred_element_type=F32)
        for g, h in enumerate(heads):
            o_ref[:, h * HEAD_DIM:(h + 1) * HEAD_DIM] = o[g * BLOCK:(g + 1) * BLOCK].astype(o_ref.dtype)


def _attn_prompt(qkv, sinks, *, n_batch, seq):
    nb = seq // BLOCK
    k_col = Q_WIDTH // KV_WIDTH
    cur = lambda col: (lambda b, n: (b * nb + n, col))
    prev = lambda col: (lambda b, n: (jnp.maximum(b * nb + n - 1, 0), col))
    kv = lambda imap: pl.BlockSpec((BLOCK, KV_WIDTH), imap)
    return pl.pallas_call(
        _attn_prompt_body,
        grid=(n_batch, nb),
        in_specs=[pl.BlockSpec((BLOCK, Q_WIDTH), cur(0)),
                  kv(cur(k_col)), kv(prev(k_col)), kv(cur(k_col + 1)), kv(prev(k_col + 1)),
                  pl.BlockSpec((N_HEADS, HEAD_DIM), lambda b, n: (0, 0))],
        out_specs=pl.BlockSpec((BLOCK, Q_WIDTH), cur(0)),
        out_shape=jax.ShapeDtypeStruct((n_batch * seq, Q_WIDTH), BF16),
        compiler_params=_params(2),
        name="attn_prompt",
    )(qkv, qkv, qkv, qkv, qkv, sinks)


def _attn_sample_body(qkv_ref, ck_ref, cv_ref, sink_ref, o_ref, ko_ref, vo_ref, *, bs):
    n_rows = WINDOW * N_KV_HEADS
    keep = n_rows - N_KV_HEADS
    k_row = N_HEADS
    v_row = N_HEADS + N_KV_HEADS
    head = lax.broadcasted_iota(jnp.int32, (N_HEADS, n_rows), 0)
    col = lax.broadcasted_iota(jnp.int32, (N_HEADS, n_rows), 1)
    same_group = (col % N_KV_HEADS) == (head // GROUP)
    sink = sink_ref[:, 0:1]

    def per_sample(b, carry):
        ko_ref[b, 0:keep, :] = ck_ref[b, N_KV_HEADS:n_rows, :]
        ko_ref[b, keep:n_rows, :] = qkv_ref[b, k_row:k_row + N_KV_HEADS, :]
        vo_ref[b, 0:keep, :] = cv_ref[b, N_KV_HEADS:n_rows, :]
        vo_ref[b, keep:n_rows, :] = qkv_ref[b, v_row:v_row + N_KV_HEADS, :]
        q = qkv_ref[b, 0:N_HEADS, :].astype(BF16)
        kk = ko_ref[b].astype(BF16)
        vv = vo_ref[b].astype(BF16)
        s = lax.dot_general(q, kk, (((1,), (1,)), ((), ())), preferred_element_type=F32) * SCALE
        p = _softmax_with_sink(jnp.where(same_group, s, -jnp.inf), sink)
        o_ref[b] = jnp.dot(p.astype(BF16), vv, preferred_element_type=F32).astype(o_ref.dtype)
        return carry

    lax.fori_loop(0, bs, per_sample, 0)


def _attn_sample(qkv, cache_k, cache_v, sinks, *, bs):
    n = qkv.shape[0]
    n_rows = QKV_WIDTH // HEAD_DIM
    cache = pl.BlockSpec((bs, WINDOW * N_KV_HEADS, HEAD_DIM), lambda b: (b, 0, 0))
    o, k_new, v_new = pl.pallas_call(
        functools.partial(_attn_sample_body, bs=bs),
        grid=(n // bs,),
        in_specs=[pl.BlockSpec((bs, n_rows, HEAD_DIM), lambda b: (b, 0, 0)), cache, cache,
                  pl.BlockSpec((N_HEADS, HEAD_DIM), lambda b: (0, 0))],
        out_specs=[pl.BlockSpec((bs, N_HEADS, HEAD_DIM), lambda b: (b, 0, 0)), cache, cache],
        out_shape=[jax.ShapeDtypeStruct((n, N_HEADS, HEAD_DIM), BF16),
                   jax.ShapeDtypeStruct(cache_k.shape, F32), jax.ShapeDtypeStruct(cache_v.shape, F32)],
        compiler_params=_params(1),
        name="attn_sample",
    )(qkv.reshape(n, n_rows, HEAD_DIM), cache_k, cache_v, sinks)
    return o.reshape(n, Q_WIDTH), k_new, v_new


def _ffn_body(xp_ref, xs_ref, wg_ref, wv_ref, cwg_ref, cwv_ref, cbg_ref, cbv_ref,
              h0g_ref, h0v_ref, h1g_ref, h1v_ref,
              fp_ref, tg_ref, tv_ref, fs_ref, rg_ref, rv_ref,
              wg_bf, wv_bf, eg_ref, ev_ref, *, tm, tiles_per_seq):
    i = pl.program_id(1)
    halves = ((wg_bf, eg_ref, cwg_ref, cbg_ref, tg_ref, h0g_ref, h1g_ref, rg_ref),
              (wv_bf, ev_ref, cwv_ref, cbv_ref, tv_ref, h0v_ref, h1v_ref, rv_ref))

    @pl.when(i == 0)
    def _new_column_block():
        _cast_weights((wg_ref, wv_ref), (wg_bf, wv_bf))
        x = xs_ref[...]
        ys = []
        for w_bf, _, cw_ref, cb_ref, _, h0_ref, h1_ref, r_ref in halves:
            d = jnp.dot(x, w_bf[...], preferred_element_type=F32)
            r_ref[...] = d
            ys.append(cw_ref[0:1, :] * h0_ref[...] + cw_ref[1:2, :] * h1_ref[...] + cw_ref[2:3, :] * d
                      + cb_ref[...])
        fs_ref[...] = (ys[0] * _sigmoid(ys[0]) * ys[1]).astype(fs_ref.dtype)

    @pl.when(i % tiles_per_seq == 0)
    def _zero_history():
        eg_ref[0:SUBLANES, :] = jnp.zeros((SUBLANES, eg_ref.shape[1]), F32)
        ev_ref[0:SUBLANES, :] = jnp.zeros((SUBLANES, ev_ref.shape[1]), F32)

    for rows in _row_chunks(tm):
        x = xp_ref[rows, :]
        n = rows.stop - rows.start
        ys = []
        for w_bf, e_ref, cw_ref, cb_ref, _, _, _, _ in halves:
            e_ref[SUBLANES + rows.start:SUBLANES + rows.stop, :] = jnp.dot(x, w_bf[...], preferred_element_type=F32)
            y = cb_ref[...]
            for k in range(FFN_CONV_K):
                r0 = rows.start + SUBLANES - (FFN_CONV_K - 1) + k
                y = y + cw_ref[k:k + 1, :] * e_ref[r0:r0 + n, :]
            ys.append(y)
        fp_ref[rows, :] = (ys[0] * _sigmoid(ys[0]) * ys[1]).astype(fp_ref.dtype)
    for _, e_ref, _, _, t_ref, _, _, _ in halves:
        tail = e_ref[tm:tm + SUBLANES, :]
        t_ref[0] = tail
        e_ref[0:SUBLANES, :] = tail


def _ffn_up(x_p, x_s, w_up, conv_w, conv_b, hist, *, tm, tn, seq):
    m, d = x_p.shape
    m_s = x_s.shape[0]
    half_blocks = D_FF // tn
    tiles_per_seq = seq // tm
    col = lambda off: (lambda j, i: (0, off + j))
    conv_b = conv_b.reshape(1, 2 * D_FF)
    in_specs = [pl.BlockSpec((tm, d), lambda j, i: (i, 0)),
                pl.BlockSpec((m_s, d), lambda j, i: (0, 0)),
                pl.BlockSpec((d, tn), col(0)), pl.BlockSpec((d, tn), col(half_blocks)),
                pl.BlockSpec((FFN_CONV_K, tn), col(0)), pl.BlockSpec((FFN_CONV_K, tn), col(half_blocks)),
                pl.BlockSpec((1, tn), col(0)), pl.BlockSpec((1, tn), col(half_blocks))]
    in_specs += [pl.BlockSpec((m_s, tn), col(k * half_blocks)) for k in range(4)]
    args = [x_p, x_s, w_up, w_up, conv_w, conv_w, conv_b, conv_b] + [hist] * 4
    tail_spec = pl.BlockSpec((1, SUBLANES, tn), lambda j, i: (i // tiles_per_seq, 0, j))
    tail_shape = jax.ShapeDtypeStruct((m // seq, SUBLANES, D_FF), F32)
    s_spec = pl.BlockSpec((m_s, tn), lambda j, i: (0, j))
    return pl.pallas_call(
        functools.partial(_ffn_body, tm=tm, tiles_per_seq=tiles_per_seq),
        grid=(half_blocks, m // tm),
        in_specs=in_specs,
        out_specs=[pl.BlockSpec((tm, tn), lambda j, i: (i, j)), tail_spec, tail_spec, s_spec, s_spec, s_spec],
        out_shape=[jax.ShapeDtypeStruct((m, D_FF), BF16), tail_shape, tail_shape,
                   jax.ShapeDtypeStruct((m_s, D_FF), BF16),
                   jax.ShapeDtypeStruct((m_s, D_FF), F32), jax.ShapeDtypeStruct((m_s, D_FF), F32)],
        scratch_shapes=[pltpu.VMEM((d, tn), BF16)] * 2 + [pltpu.VMEM((tm + SUBLANES, tn), F32)] * 2,
        compiler_params=_params(2),
        name="ffn_up",
    )(*args)


def kernel(x_prompt, x_sample, cache_k_win, cache_v_win, state_conv_a, state_conv_ffn, norm1_g, w_in,
           conv_a_w, conv_a_b, ln_a_g, ln_a_b, w_a_out, attn_sinks, w_attn_out, w_o, norm2_g, w_up,
           conv_f_w, conv_f_b, w_down, norm_f_g):
    n_batch, seq, d = x_prompt.shape
    n_dec = x_sample.shape[0]
    assert x_sample.shape[1] == 1 and w_in.shape[0] == 1
    w_in, w_a_out, w_attn_out, w_o, w_down = w_in[0], w_a_out[0], w_attn_out[0], w_o[0], w_down[0]
    sinks = jnp.broadcast_to(attn_sinks[0][:, None], (N_HEADS, HEAD_DIM))
    conv_a_args = (conv_a_w[0], conv_a_b[0], ln_a_g[0], ln_a_b[0])
    tm, tr = 1024, 256
    tile = lambda j, i: (i, j)
    full = lambda arrs: (arrs, 0, arrs[0].shape[1])

    x = (x_prompt.reshape(n_batch * seq, d), x_sample.reshape(n_dec, d))
    xn = tuple(_rmsnorm(a, norm1_g[0], BF16, tr) for a in x)
    (u,) = _ws_matmul([full(xn)], [(0, w_in, 0), (0, w_in, CONV_WIDTH)], [], _glu_epilogue, [F32],
                      n_cols=CONV_WIDTH, tm=tm, tn=256, name="in_glu")
    cos_p, sin_p = _rope_tables(seq, 0, 1)
    cos_s, sin_s = _rope_tables(n_dec, PAST_LEN, 0)
    rope_map = lambda j, i: (i % (seq // tm), 0)
    tn_qkv = KV_WIDTH
    (qkv,) = _ws_matmul([full(xn)], [(0, w_in, A_END)],
                        [((cos_p, cos_s), HEAD_DIM, rope_map), ((sin_p, sin_s), HEAD_DIM, rope_map)],
                        functools.partial(_rope_epilogue, tn=tn_qkv), [F32],
                        n_cols=QKV_WIDTH, tm=tm, tn=tn_qkv, name="in_qkv_rope")

    c_p = _conv_a_prompt(u[0], *conv_a_args, n_batch=n_batch, seq=seq, tl=256)
    hist_a = jnp.transpose(state_conv_a[0], (1, 0, 2))
    c_s, hist_a_new = _conv_a_sample(hist_a, u[1], *conv_a_args, bs=8)

    o_p = _attn_prompt(qkv[0], sinks, n_batch=n_batch, seq=seq)
    cache_shape = (n_dec, WINDOW * N_KV_HEADS, HEAD_DIM)
    o_s, k_win_s, v_win_s = _attn_sample(qkv[1], cache_k_win[0].reshape(cache_shape),
                                         cache_v_win[0].reshape(cache_shape), sinks, bs=8)

    (merged,) = _ws_matmul([full(xn), full((c_p, c_s)), full((o_p, o_s))],
                           [(0, w_in, V_END), (1, w_a_out, 0), (0, w_in, V_END + D_MODEL), (2, w_attn_out, 0)],
                           [], _merge_epilogue, [BF16],
                           n_cols=D_MODEL, tm=512, tn=256, name="gated_merge")
    (h,) = _ws_matmul([full(merged)], [(0, w_o, 0)], [(x, 512, tile)], _residual_epilogue, [F32],
                      n_cols=D_MODEL, tm=tm, tn=512, name="out_proj")

    hn = tuple(_rmsnorm(a, norm2_g[0], BF16, tr) for a in h)
    ffn_hist = state_conv_ffn[0].reshape(n_dec, (FFN_CONV_K - 1) * 2 * D_FF)
    f_p, tail_g, tail_v, f_s, raw_g, raw_v = _ffn_up(hn[0], hn[1], w_up[0], conv_f_w[0], conv_f_b[0], ffn_hist,
                                                     tm=tm, tn=256, seq=seq)
    y = h
    k_half = D_FF // 2
    for kb in range(2):
        (y,) = _ws_matmul([((f_p, f_s), kb, k_half)], [(0, w_down, 0)], [(y, 512, tile)], _residual_epilogue,
                          [F32], n_cols=D_MODEL, tm=512, tn=512, name="ffn_down")
    y_p, y_s = (_rmsnorm(a, norm_f_g, F32, tr) for a in y)

    qkv_p = qkv[0].reshape(n_batch, seq, QKV_WIDTH)
    win_shape = (1, n_batch, WINDOW, N_KV_HEADS, HEAD_DIM)
    k_win_p = qkv_p[:, seq - WINDOW:, Q_WIDTH:Q_WIDTH + KV_WIDTH].reshape(win_shape)
    v_win_p = qkv_p[:, seq - WINDOW:, Q_WIDTH + KV_WIDTH:].reshape(win_shape)
    conv_a_p = u[0].reshape(n_batch, seq, CONV_WIDTH)[None, :, seq - (CONV_K - 1):]
    keep = SUBLANES - (FFN_CONV_K - 1)
    conv_ffn_p = jnp.concatenate([tail_g[:, keep:], tail_v[:, keep:]], axis=-1)[None]
    conv_a_s = jnp.transpose(hist_a_new, (1, 0, 2))[None]
    up_new = jnp.concatenate([raw_g, raw_v], axis=-1)[:, None]
    conv_ffn_s = jnp.concatenate([state_conv_ffn[0][:, 1:], up_new], axis=1)[None]
    return (y_p.reshape(n_batch, seq, d), y_s.reshape(n_dec, 1, d), k_win_p, v_win_p, conv_a_p, conv_ffn_p,
            k_win_s.reshape(cache_k_win.shape), v_win_s.reshape(cache_v_win.shape), conv_a_s, conv_ffn_s)
```

```python
import functools

import jax
import jax.numpy as jnp
from jax import lax
from jax.experimental import pallas as pl
from jax.experimental.pallas import tpu as pltpu

F32 = jnp.float32
BF16 = jnp.bfloat16

D_MODEL = 4096
CONV_WIDTH = D_MODEL // 2
CONV_K = 31
HEAD_DIM = 128
N_HEADS = (D_MODEL // 2) // HEAD_DIM
N_KV_HEADS = N_HEADS // 4
GROUP = N_HEADS // N_KV_HEADS
ROT_DIM = HEAD_DIM // 4
ROPE_THETA = 500000.0
WINDOW = 128
BLOCK = 128
D_FF = 11008
FFN_CONV_K = 3
EPS = 1e-6
PAST_LEN = 8192
Q_WIDTH = N_HEADS * HEAD_DIM
KV_WIDTH = N_KV_HEADS * HEAD_DIM
A_END = 2 * CONV_WIDTH
V_END = A_END + Q_WIDTH + 2 * KV_WIDTH
QKV_WIDTH = Q_WIDTH + 2 * KV_WIDTH
SCALE = HEAD_DIM ** -0.5

V7X_VMEM_BYTES = 64 * 1024 * 1024
VMEM_LIMIT = V7X_VMEM_BYTES - 8 * 1024 * 1024
SUBLANES = 8
CONV_HALO = 32
ROW_CHUNK = 128


def _params(n_axes):
    return pltpu.CompilerParams(dimension_semantics=("arbitrary",) * n_axes,
                                vmem_limit_bytes=VMEM_LIMIT)


def _sigmoid(x):
    return 1.0 / (1.0 + jnp.exp(-x))


def _row_chunks(tm):
    rc = min(tm, ROW_CHUNK)
    return [slice(r * rc, (r + 1) * rc) for r in range(tm // rc)]


def _rmsnorm_body(x_ref, g_ref, o_ref):
    x = x_ref[...]
    ms = jnp.mean(x * x, axis=-1, keepdims=True)
    o_ref[...] = (x * lax.rsqrt(ms + EPS) * g_ref[...]).astype(o_ref.dtype)


def _rmsnorm(x, g, out_dtype, tr):
    m, d = x.shape
    tr = min(tr, m)
    return pl.pallas_call(
        _rmsnorm_body,
        grid=(m // tr,),
        in_specs=[pl.BlockSpec((tr, d), lambda i: (i, 0)),
                  pl.BlockSpec((1, d), lambda i: (0, 0))],
        out_specs=pl.BlockSpec((tr, d), lambda i: (i, 0)),
        out_shape=jax.ShapeDtypeStruct((m, d), out_dtype),
        compiler_params=_params(1),
        name="rmsnorm",
    )(x, g.reshape(1, d))


def _rope_table_body(inv_ref, sign_ref, cos_ref, sin_ref, *, pos0, pos_step):
    rows = cos_ref.shape[0]
    i = pl.program_id(0)
    r = lax.broadcasted_iota(jnp.int32, (rows, HEAD_DIM), 0) + i * rows
    lane = lax.broadcasted_iota(jnp.int32, (rows, HEAD_DIM), 1)
    pos = (pos0 + pos_step * r).astype(F32)
    ang = pos * inv_ref[...]
    cos_ref[...] = jnp.where(lane < ROT_DIM, jnp.cos(ang), 1.0)
    sin_ref[...] = sign_ref[...] * jnp.sin(ang)


def _rope_tables(n_rows, pos0, pos_step):
    half = ROT_DIM // 2
    inv = ROPE_THETA ** (-2.0 * jnp.arange(half, dtype=F32) / ROT_DIM)
    zeros = jnp.zeros((HEAD_DIM - ROT_DIM,), F32)
    inv_full = jnp.concatenate([inv, inv, zeros]).reshape(1, HEAD_DIM)
    sign = jnp.concatenate([-jnp.ones((half,), F32), jnp.ones((half,), F32), zeros]).reshape(1, HEAD_DIM)
    tr = min(n_rows, 512)
    spec_c = pl.BlockSpec((1, HEAD_DIM), lambda i: (0, 0))
    spec_t = pl.BlockSpec((tr, HEAD_DIM), lambda i: (i, 0))
    return pl.pallas_call(
        functools.partial(_rope_table_body, pos0=pos0, pos_step=pos_step),
        grid=(n_rows // tr,),
        in_specs=[spec_c, spec_c],
        out_specs=[spec_t, spec_t],
        out_shape=[jax.ShapeDtypeStruct((n_rows, HEAD_DIM), F32)] * 2,
        compiler_params=_params(1),
        name="rope_tables",
    )(inv_full, sign)


def _cast_weights(w_refs, wbf_refs):
    for w_ref, s_ref in zip(w_refs, wbf_refs):
        s_ref[...] = w_ref[...].astype(BF16)


def _ws_body(*refs, dot_lhs, n_lhs, n_extra, n_out, tms, epilogue):
    n_dots = len(dot_lhs)
    pos = 0
    def take(n):
        nonlocal pos
        out = refs[pos:pos + n]
        pos += n
        return out
    lhs_p, lhs_s = take(n_lhs), take(n_lhs)
    w_refs = take(n_dots)
    ex_p, ex_s = take(n_extra), take(n_extra)
    out_p, out_s = take(n_out), take(n_out)
    wbf_refs = take(n_dots)
    stage_bufs = (take(n_dots), take(n_dots))
    j = pl.program_id(0)
    i = pl.program_id(1)

    def run_rows(lhs_refs, ex_refs, out_refs, tm):
        chunks = _row_chunks(tm)
        n = chunks[0].stop
        for s in range(len(chunks) + 1):
            if s < len(chunks):
                for li, s_ref, r_ref in zip(dot_lhs, wbf_refs, stage_bufs[s % 2]):
                    r_ref[0:n, :] = jnp.dot(lhs_refs[li][chunks[s], :], s_ref[...], preferred_element_type=F32)
            if s >= 1:
                rows = chunks[s - 1]
                accs = [r_ref[0:n, :] for r_ref in stage_bufs[(s - 1) % 2]]
                for o_ref, val in zip(out_refs, epilogue(accs, ex_refs, rows, j)):
                    o_ref[rows, :] = val.astype(o_ref.dtype)

    @pl.when(i == 0)
    def _new_column_block():
        _cast_weights(w_refs, wbf_refs)
        run_rows(lhs_s, ex_s, out_s, tms[1])

    run_rows(lhs_p, ex_p, out_p, tms[0])


def _ws_matmul(lhs, dots, extras, epilogue, out_dtypes, *, n_cols, tm, tn, name):
    m_p = lhs[0][0][0].shape[0]
    m_s = lhs[0][0][1].shape[0]
    in_specs, args = [], []
    for grp in (0, 1):
        for arrs, kb, ks in lhs:
            if grp == 0:
                in_specs.append(pl.BlockSpec((tm, ks), lambda j, i, kb=kb: (i, kb)))
            else:
                in_specs.append(pl.BlockSpec((m_s, ks), lambda j, i, kb=kb: (0, kb)))
            args.append(arrs[grp])
    for li, w, off in dots:
        _, kb, ks = lhs[li]
        in_specs.append(pl.BlockSpec((ks, tn), lambda j, i, kb=kb, ob=off // tn: (kb, ob + j)))
        args.append(w)
    for grp in (0, 1):
        for arrs, cols, imap in extras:
            if grp == 0:
                in_specs.append(pl.BlockSpec((tm, cols), imap))
            else:
                in_specs.append(pl.BlockSpec((m_s, cols), (lambda j, i: (0, j)) if cols == tn else (lambda j, i: (0, 0))))
            args.append(arrs[grp])
    out_specs = ([pl.BlockSpec((tm, tn), lambda j, i: (i, j)) for _ in out_dtypes]
                 + [pl.BlockSpec((m_s, tn), lambda j, i: (0, j)) for _ in out_dtypes])
    out_shape = ([jax.ShapeDtypeStruct((m_p, n_cols), dt) for dt in out_dtypes]
                 + [jax.ShapeDtypeStruct((m_s, n_cols), dt) for dt in out_dtypes])
    body = functools.partial(_ws_body, dot_lhs=tuple(l for l, _, _ in dots), n_lhs=len(lhs),
                             n_extra=len(extras), n_out=len(out_dtypes), tms=(tm, m_s), epilogue=epilogue)
    outs = pl.pallas_call(
        body,
        grid=(n_cols // tn, m_p // tm),
        in_specs=in_specs,
        out_specs=out_specs,
        out_shape=out_shape,
        scratch_shapes=([pltpu.VMEM((lhs[li][2], tn), BF16) for li, _, _ in dots]
                        + [pltpu.VMEM((min(tm, ROW_CHUNK), tn), F32) for _ in range(2 * len(dots))]),
        compiler_params=_params(2),
        name=name,
    )(*args)
    n = len(out_dtypes)
    return list(zip(outs[:n], outs[n:]))


def _glu_epilogue(accs, ex_refs, rows, j):
    return [accs[0] * _sigmoid(accs[1])]


def _rope_epilogue(accs, ex_refs, rows, j, *, tn):
    cos_ref, sin_ref = ex_refs
    z = accs[0]
    rotated = j < (Q_WIDTH + KV_WIDTH) // tn
    cos_t = jnp.where(rotated, cos_ref[rows, :], 1.0)
    sin_t = jnp.where(rotated, sin_ref[rows, :], 0.0)
    lane = lax.broadcasted_iota(jnp.int32, cos_t.shape, 1)
    half = ROT_DIM // 2
    outs = []
    for c in range(tn // HEAD_DIM):
        x = z[:, c * HEAD_DIM:(c + 1) * HEAD_DIM]
        partner = jnp.where(lane < half, pltpu.roll(x, HEAD_DIM - half, 1), pltpu.roll(x, half, 1))
        outs.append(x * cos_t + partner * sin_t)
    return [jnp.concatenate(outs, axis=1)]


def _merge_epilogue(accs, ex_refs, rows, j):
    gate_a, br_a, gate_b, br_b = accs
    return [_sigmoid(gate_a) * br_a + _sigmoid(gate_b) * br_b]


def _residual_epilogue(accs, ex_refs, rows, j):
    return [ex_refs[0][rows, :] + accs[0]]


def _ln_swish(y, g_ref, b_ref):
    mu = jnp.mean(y, axis=-1, keepdims=True)
    yc = y - mu
    var = jnp.mean(yc * yc, axis=-1, keepdims=True)
    t = yc * lax.rsqrt(var + EPS) * g_ref[...] + b_ref[...]
    return t * _sigmoid(t)


def _conv_a_prompt_body(cur_ref, prev_ref, w_ref, b_ref, g_ref, beta_ref, o_ref, xx_ref, sh_ref, y_ref, *, tl):
    first = pl.program_id(1) == 0
    xx_ref[0:CONV_HALO, :] = jnp.where(first, 0.0, prev_ref[...])
    xx_ref[CONV_HALO:, :] = cur_ref[...]
    n_sh = sh_ref.shape[1]
    for b in range(1, SUBLANES):
        sh_ref[b - 1] = xx_ref[b:b + n_sh, :]
    rows, lanes = 32, 512
    base = CONV_HALO - (CONV_K - 1)

    def row_chunk(rc, carry):
        r0 = pl.multiple_of(rc * rows, rows)
        for cc in range(CONV_WIDTH // lanes):
            cs = slice(cc * lanes, (cc + 1) * lanes)
            acc = jnp.zeros((rows, lanes), F32) + b_ref[:, cs]
            for k in range(CONV_K):
                a, b = divmod(base + k, SUBLANES)
                start = r0 + a * SUBLANES
                if b == 0:
                    x = xx_ref[pl.ds(start, rows), cs]
                else:
                    x = sh_ref[b - 1, pl.ds(start, rows), cs]
                acc = acc + w_ref[k:k + 1, cs] * x
            y_ref[pl.ds(r0, rows), cs] = acc
        return carry

    lax.fori_loop(0, tl // rows, row_chunk, 0)
    o_ref[...] = _ln_swish(y_ref[...], g_ref, beta_ref).astype(o_ref.dtype)


def _conv_a_prompt(u, conv_w, conv_b, ln_g, ln_b, *, n_batch, seq, tl):
    c = CONV_WIDTH
    nl = seq // tl
    halo_per_tile = tl // CONV_HALO
    row = lambda a: a.reshape(1, c)
    vec = pl.BlockSpec((1, c), lambda b, l: (0, 0))
    return pl.pallas_call(
        functools.partial(_conv_a_prompt_body, tl=tl),
        grid=(n_batch, nl),
        in_specs=[pl.BlockSpec((tl, c), lambda b, l: (b * nl + l, 0)),
                  pl.BlockSpec((CONV_HALO, c),
                               lambda b, l: (jnp.maximum((b * nl + l) * halo_per_tile - 1, 0), 0)),
                  pl.BlockSpec((CONV_K, c), lambda b, l: (0, 0)),
                  vec, vec, vec],
        out_specs=pl.BlockSpec((tl, c), lambda b, l: (b * nl + l, 0)),
        out_shape=jax.ShapeDtypeStruct((n_batch * seq, c), BF16),
        scratch_shapes=[pltpu.VMEM((tl + CONV_HALO, c), F32),
                        pltpu.VMEM((SUBLANES - 1, tl + CONV_HALO - SUBLANES, c), F32),
                        pltpu.VMEM((tl, c), F32)],
        compiler_params=_params(2),
        name="conv_a_prompt",
    )(u, u, conv_w, row(conv_b), row(ln_g), row(ln_b))


def _conv_a_sample_body(hist_ref, u_ref, w_ref, b_ref, g_ref, beta_ref, o_ref, new_ref):
    u = u_ref[...]
    y = u * w_ref[CONV_K - 1:CONV_K, :] + b_ref[...]
    for k in range(CONV_K - 1):
        y = y + hist_ref[k] * w_ref[k:k + 1, :]
    o_ref[...] = _ln_swish(y, g_ref, beta_ref).astype(o_ref.dtype)
    new_ref[0:CONV_K - 2] = hist_ref[1:CONV_K - 1]
    new_ref[CONV_K - 2] = u


def _conv_a_sample(hist_t, u, conv_w, conv_b, ln_g, ln_b, *, bs):
    n, c = u.shape
    row = lambda a: a.reshape(1, c)
    vec = pl.BlockSpec((1, c), lambda b: (0, 0))
    hist_spec = pl.BlockSpec((CONV_K - 1, bs, c), lambda b: (0, b, 0))
    return pl.pallas_call(
        _conv_a_sample_body,
        grid=(n // bs,),
        in_specs=[hist_spec,
                  pl.BlockSpec((bs, c), lambda b: (b, 0)),
                  pl.BlockSpec((CONV_K, c), lambda b: (0, 0)),
                  vec, vec, vec],
        out_specs=[pl.BlockSpec((bs, c), lambda b: (b, 0)), hist_spec],
        out_shape=[jax.ShapeDtypeStruct((n, c), BF16), jax.ShapeDtypeStruct(hist_t.shape, F32)],
        compiler_params=_params(1),
        name="conv_a_sample",
    )(hist_t, u, conv_w, row(conv_b), row(ln_g), row(ln_b))


def _softmax_with_sink(s, sink):
    m = jnp.maximum(jnp.max(s, axis=-1, keepdims=True), sink)
    p = jnp.exp(s - m)
    return p / (jnp.sum(p, axis=-1, keepdims=True) + jnp.exp(sink - m))


def _attn_prompt_body(q_ref, kc_ref, kp_ref, vc_ref, vp_ref, sink_ref, o_ref):
    n = pl.program_id(1)
    rows = GROUP * BLOCK
    r = lax.broadcasted_iota(jnp.int32, (rows, 2 * BLOCK), 0) % BLOCK
    c = lax.broadcasted_iota(jnp.int32, (rows, 2 * BLOCK), 1)
    diff = (r + BLOCK) - c
    mask = (diff >= 0) & (diff < WINDOW) & ((c >= BLOCK) | (n > 0))
    for kh in range(N_KV_HEADS):
        ks = slice(kh * HEAD_DIM, (kh + 1) * HEAD_DIM)
        kk = jnp.concatenate([kp_ref[:, ks], kc_ref[:, ks]], axis=0).astype(BF16)
        vv = jnp.concatenate([vp_ref[:, ks], vc_ref[:, ks]], axis=0).astype(BF16)
        heads = [kh * GROUP + g for g in range(GROUP)]
        qg = jnp.concatenate([q_ref[:, h * HEAD_DIM:(h + 1) * HEAD_DIM] for h in heads],
                             axis=0).astype(BF16)
        sink = jnp.concatenate([jnp.broadcast_to(sink_ref[h:h + 1, 0:1], (BLOCK, 1)) for h in heads],
                               axis=0)
        s = lax.dot_general(qg, kk, (((1,), (1,)), ((), ())), preferred_element_type=F32) * SCALE
        p = _softmax_with_sink(jnp.where(mask, s, -jnp.inf), sink)
        o = jnp.dot(p.astype(BF16), vv, preferred_element_type=F32)
        for g, h in enumerate(heads):
            o_ref[:, h * HEAD_DIM:(h + 1) * HEAD_DIM] = o[g * BLOCK:(g + 1) * BLOCK].astype(o_ref.dtype)


def _attn_prompt(qkv, sinks, *, n_batch, seq):
    nb = seq // BLOCK
    k_col = Q_WIDTH // KV_WIDTH
    cur = lambda col: (lambda b, n: (b * nb + n, col))
    prev = lambda col: (lambda b, n: (jnp.maximum(b * nb + n - 1, 0), col))
    kv = lambda imap: pl.BlockSpec((BLOCK, KV_WIDTH), imap)
    return pl.pallas_call(
        _attn_prompt_body,
        grid=(n_batch, nb),
        in_specs=[pl.BlockSpec((BLOCK, Q_WIDTH), cur(0)),
                  kv(cur(k_col)), kv(prev(k_col)), kv(cur(k_col + 1)), kv(prev(k_col + 1)),
                  pl.BlockSpec((N_HEADS, HEAD_DIM), lambda b, n: (0, 0))],
        out_specs=pl.BlockSpec((BLOCK, Q_WIDTH), cur(0)),
        out_shape=jax.ShapeDtypeStruct((n_batch * seq, Q_WIDTH), BF16),
        compiler_params=_params(2),
        name="attn_prompt",
    )(qkv, qkv, qkv, qkv, qkv, sinks)


def _attn_sample_body(qkv_ref, ck_ref, cv_ref, sink_ref, o_ref, ko_ref, vo_ref, *, bs):
    n_rows = WINDOW * N_KV_HEADS
    keep = n_rows - N_KV_HEADS
    k_row = N_HEADS
    v_row = N_HEADS + N_KV_HEADS
    head = lax.broadcasted_iota(jnp.int32, (N_HEADS, n_rows), 0)
    col = lax.broadcasted_iota(jnp.int32, (N_HEADS, n_rows), 1)
    same_group = (col % N_KV_HEADS) == (head // GROUP)
    sink = sink_ref[:, 0:1]

    def per_sample(b, carry):
        ko_ref[b, 0:keep, :] = ck_ref[b, N_KV_HEADS:n_rows, :]
        ko_ref[b, keep:n_rows, :] = qkv_ref[b, k_row:k_row + N_KV_HEADS, :]
        vo_ref[b, 0:keep, :] = cv_ref[b, N_KV_HEADS:n_rows, :]
        vo_ref[b, keep:n_rows, :] = qkv_ref[b, v_row:v_row + N_KV_HEADS, :]
        q = qkv_ref[b, 0:N_HEADS, :].astype(BF16)
        kk = ko_ref[b].astype(BF16)
        vv = vo_ref[b].astype(BF16)
        s = lax.dot_general(q, kk, (((1,), (1,)), ((), ())), preferred_element_type=F32) * SCALE
        p = _softmax_with_sink(jnp.where(same_group, s, -jnp.inf), sink)
        o_ref[b] = jnp.dot(p.astype(BF16), vv, preferred_element_type=F32).astype(o_ref.dtype)
        return carry

    lax.fori_loop(0, bs, per_sample, 0)


def _attn_sample(qkv, cache_k, cache_v, sinks, *, bs):
    n = qkv.shape[0]
    n_rows = QKV_WIDTH // HEAD_DIM
    cache = pl.BlockSpec((bs, WINDOW * N_KV_HEADS, HEAD_DIM), lambda b: (b, 0, 0))
    o, k_new, v_new = pl.pallas_call(
        functools.partial(_attn_sample_body, bs=bs),
        grid=(n // bs,),
        in_specs=[pl.BlockSpec((bs, n_rows, HEAD_DIM), lambda b: (b, 0, 0)), cache, cache,
                  pl.BlockSpec((N_HEADS, HEAD_DIM), lambda b: (0, 0))],
        out_specs=[pl.BlockSpec((bs, N_HEADS, HEAD_DIM), lambda b: (b, 0, 0)), cache, cache],
        out_shape=[jax.ShapeDtypeStruct((n, N_HEADS, HEAD_DIM), BF16),
                   jax.ShapeDtypeStruct(cache_k.shape, F32), jax.ShapeDtypeStruct(cache_v.shape, F32)],
        compiler_params=_params(1),
        name="attn_sample",
    )(qkv.reshape(n, n_rows, HEAD_DIM), cache_k, cache_v, sinks)
    return o.reshape(n, Q_WIDTH), k_new, v_new


def _ffn_body(xp_ref, xs_ref, wg_ref, wv_ref, cwg_ref, cwv_ref, cbg_ref, cbv_ref,
              h0g_ref, h0v_ref, h1g_ref, h1v_ref,
              fp_ref, tg_ref, tv_ref, fs_ref, rg_ref, rv_ref,
              wg_bf, wv_bf, eg0_ref, eg1_ref, ev0_ref, ev1_ref, *, tm, tiles_per_seq):
    i = pl.program_id(1)
    chunks = _row_chunks(tm)
    rc = chunks[0].stop
    assert len(chunks) % 2 == 0
    stage_bufs = ((eg0_ref, ev0_ref), (eg1_ref, ev1_ref))
    last_bufs = stage_bufs[(len(chunks) - 1) % 2]
    head = slice(0, SUBLANES)
    tail = slice(rc, rc + SUBLANES)

    @pl.when(i == 0)
    def _new_column_block():
        _cast_weights((wg_ref, wv_ref), (wg_bf, wv_bf))
        x = xs_ref[...]
        ys = []
        for w_bf, cw_ref, cb_ref, h0_ref, h1_ref, r_ref in ((wg_bf, cwg_ref, cbg_ref, h0g_ref, h1g_ref, rg_ref),
                                                            (wv_bf, cwv_ref, cbv_ref, h0v_ref, h1v_ref, rv_ref)):
            d = jnp.dot(x, w_bf[...], preferred_element_type=F32)
            r_ref[...] = d
            ys.append(cw_ref[0:1, :] * h0_ref[...] + cw_ref[1:2, :] * h1_ref[...] + cw_ref[2:3, :] * d
                      + cb_ref[...])
        fs_ref[...] = (ys[0] * _sigmoid(ys[0]) * ys[1]).astype(fs_ref.dtype)

    @pl.when(i % tiles_per_seq == 0)
    def _zero_history():
        for e_ref in last_bufs:
            e_ref[tail, :] = jnp.zeros((SUBLANES, e_ref.shape[1]), F32)

    def conv(e_ref, cw_ref, cb_ref):
        y = cb_ref[...]
        for k in range(FFN_CONV_K):
            r0 = SUBLANES - (FFN_CONV_K - 1) + k
            y = y + cw_ref[k:k + 1, :] * e_ref[r0:r0 + rc, :]
        return y

    for s in range(len(chunks) + 1):
        if s < len(chunks):
            x = xp_ref[chunks[s], :]
            for e_ref, e_prev, w_bf in zip(stage_bufs[s % 2], stage_bufs[(s - 1) % 2], (wg_bf, wv_bf)):
                e_ref[head, :] = e_prev[tail, :]
                e_ref[SUBLANES:, :] = jnp.dot(x, w_bf[...], preferred_element_type=F32)
        if s >= 1:
            eg_ref, ev_ref = stage_bufs[(s - 1) % 2]
            yg = conv(eg_ref, cwg_ref, cbg_ref)
            yv = conv(ev_ref, cwv_ref, cbv_ref)
            fp_ref[chunks[s - 1], :] = (yg * _sigmoid(yg) * yv).astype(fp_ref.dtype)
    for e_ref, t_ref in zip(last_bufs, (tg_ref, tv_ref)):
        t_ref[0] = e_ref[tail, :]


def _ffn_up(x_p, x_s, w_up, conv_w, conv_b, hist, *, tm, tn, seq):
    m, d = x_p.shape
    m_s = x_s.shape[0]
    half_blocks = D_FF // tn
    tiles_per_seq = seq // tm
    col = lambda off: (lambda j, i: (0, off + j))
    conv_b = conv_b.reshape(1, 2 * D_FF)
    in_specs = [pl.BlockSpec((tm, d), lambda j, i: (i, 0)),
                pl.BlockSpec((m_s, d), lambda j, i: (0, 0)),
                pl.BlockSpec((d, tn), col(0)), pl.BlockSpec((d, tn), col(half_blocks)),
                pl.BlockSpec((FFN_CONV_K, tn), col(0)), pl.BlockSpec((FFN_CONV_K, tn), col(half_blocks)),
                pl.BlockSpec((1, tn), col(0)), pl.BlockSpec((1, tn), col(half_blocks))]
    in_specs += [pl.BlockSpec((m_s, tn), col(k * half_blocks)) for k in range(4)]
    args = [x_p, x_s, w_up, w_up, conv_w, conv_w, conv_b, conv_b] + [hist] * 4
    tail_spec = pl.BlockSpec((1, SUBLANES, tn), lambda j, i: (i // tiles_per_seq, 0, j))
    tail_shape = jax.ShapeDtypeStruct((m // seq, SUBLANES, D_FF), F32)
    s_spec = pl.BlockSpec((m_s, tn), lambda j, i: (0, j))
    return pl.pallas_call(
        functools.partial(_ffn_body, tm=tm, tiles_per_seq=tiles_per_seq),
        grid=(half_blocks, m // tm),
        in_specs=in_specs,
        out_specs=[pl.BlockSpec((tm, tn), lambda j, i: (i, j)), tail_spec, tail_spec, s_spec, s_spec, s_spec],
        out_shape=[jax.ShapeDtypeStruct((m, D_FF), BF16), tail_shape, tail_shape,
                   jax.ShapeDtypeStruct((m_s, D_FF), BF16),
                   jax.ShapeDtypeStruct((m_s, D_FF), F32), jax.ShapeDtypeStruct((m_s, D_FF), F32)],
        scratch_shapes=[pltpu.VMEM((d, tn), BF16)] * 2 + [pltpu.VMEM((min(tm, ROW_CHUNK) + SUBLANES, tn), F32)] * 4,
        compiler_params=_params(2),
        name="ffn_up",
    )(*args)


def kernel(x_prompt, x_sample, cache_k_win, cache_v_win, state_conv_a, state_conv_ffn, norm1_g, w_in,
           conv_a_w, conv_a_b, ln_a_g, ln_a_b, w_a_out, attn_sinks, w_attn_out, w_o, norm2_g, w_up,
           conv_f_w, conv_f_b, w_down, norm_f_g):
    n_batch, seq, d = x_prompt.shape
    n_dec = x_sample.shape[0]
    assert x_sample.shape[1] == 1 and w_in.shape[0] == 1
    w_in, w_a_out, w_attn_out, w_o, w_down = w_in[0], w_a_out[0], w_attn_out[0], w_o[0], w_down[0]
    sinks = jnp.broadcast_to(attn_sinks[0][:, None], (N_HEADS, HEAD_DIM))
    conv_a_args = (conv_a_w[0], conv_a_b[0], ln_a_g[0], ln_a_b[0])
    tm, tr = 1024, 256
    tile = lambda j, i: (i, j)
    full = lambda arrs: (arrs, 0, arrs[0].shape[1])

    x = (x_prompt.reshape(n_batch * seq, d), x_sample.reshape(n_dec, d))
    xn = tuple(_rmsnorm(a, norm1_g[0], BF16, tr) for a in x)
    (u,) = _ws_matmul([full(xn)], [(0, w_in, 0), (0, w_in, CONV_WIDTH)], [], _glu_epilogue, [F32],
                      n_cols=CONV_WIDTH, tm=tm, tn=256, name="in_glu")
    cos_p, sin_p = _rope_tables(seq, 0, 1)
    cos_s, sin_s = _rope_tables(n_dec, PAST_LEN, 0)
    rope_map = lambda j, i: (i % (seq // tm), 0)
    tn_qkv = KV_WIDTH
    (qkv,) = _ws_matmul([full(xn)], [(0, w_in, A_END)],
                        [((cos_p, cos_s), HEAD_DIM, rope_map), ((sin_p, sin_s), HEAD_DIM, rope_map)],
                        functools.partial(_rope_epilogue, tn=tn_qkv), [F32],
                        n_cols=QKV_WIDTH, tm=tm, tn=tn_qkv, name="in_qkv_rope")

    c_p = _conv_a_prompt(u[0], *conv_a_args, n_batch=n_batch, seq=seq, tl=256)
    hist_a = jnp.transpose(state_conv_a[0], (1, 0, 2))
    c_s, hist_a_new = _conv_a_sample(hist_a, u[1], *conv_a_args, bs=8)

    o_p = _attn_prompt(qkv[0], sinks, n_batch=n_batch, seq=seq)
    cache_shape = (n_dec, WINDOW * N_KV_HEADS, HEAD_DIM)
    o_s, k_win_s, v_win_s = _attn_sample(qkv[1], cache_k_win[0].reshape(cache_shape),
                                         cache_v_win[0].reshape(cache_shape), sinks, bs=16)

    (merged,) = _ws_matmul([full(xn), full((c_p, c_s)), full((o_p, o_s))],
                           [(0, w_in, V_END), (1, w_a_out, 0), (0, w_in, V_END + D_MODEL), (2, w_attn_out, 0)],
                           [], _merge_epilogue, [BF16],
                           n_cols=D_MODEL, tm=512, tn=256, name="gated_merge")
    (h,) = _ws_matmul([full(merged)], [(0, w_o, 0)], [(x, 512, tile)], _residual_epilogue, [F32],
                      n_cols=D_MODEL, tm=tm, tn=512, name="out_proj")

    hn = tuple(_rmsnorm(a, norm2_g[0], BF16, tr) for a in h)
    ffn_hist = state_conv_ffn[0].reshape(n_dec, (FFN_CONV_K - 1) * 2 * D_FF)
    f_p, tail_g, tail_v, f_s, raw_g, raw_v = _ffn_up(hn[0], hn[1], w_up[0], conv_f_w[0], conv_f_b[0], ffn_hist,
                                                     tm=tm, tn=256, seq=seq)
    y = h
    k_half = D_FF // 2
    for kb in range(2):
        (y,) = _ws_matmul([((f_p, f_s), kb, k_half)], [(0, w_down, 0)], [(y, 512, tile)], _residual_epilogue,
                          [F32], n_cols=D_MODEL, tm=512, tn=512, name="ffn_down")
    y_p, y_s = (_rmsnorm(a, norm_f_g, F32, tr) for a in y)

    qkv_p = qkv[0].reshape(n_batch, seq, QKV_WIDTH)
    win_shape = (1, n_batch, WINDOW, N_KV_HEADS, HEAD_DIM)
    k_win_p = qkv_p[:, seq - WINDOW:, Q_WIDTH:Q_WIDTH + KV_WIDTH].reshape(win_shape)
    v_win_p = qkv_p[:, seq - WINDOW:, Q_WIDTH + KV_WIDTH:].reshape(win_shape)
    conv_a_p = u[0].reshape(n_batch, seq, CONV_WIDTH)[None, :, seq - (CONV_K - 1):]
    keep = SUBLANES - (FFN_CONV_K - 1)
    conv_ffn_p = jnp.concatenate([tail_g[:, keep:], tail_v[:, keep:]], axis=-1)[None]
    conv_a_s = jnp.transpose(hist_a_new, (1, 0, 2))[None]
    up_new = jnp.concatenate([raw_g, raw_v], axis=-1)[:, None]
    conv_ffn_s = jnp.concatenate([state_conv_ffn[0][:, 1:], up_new], axis=1)[None]
    return (y_p.reshape(n_batch, seq, d), y_s.reshape(n_dec, 1, d), k_win_p, v_win_p, conv_a_p, conv_ffn_p,
            k_win_s.reshape(cache_k_win.shape), v_win_s.reshape(cache_v_win.shape), conv_a_s, conv_ffn_s)
```

```python
import functools

import jax
import jax.numpy as jnp
from jax import lax
from jax.experimental import pallas as pl
from jax.experimental.pallas import tpu as pltpu

F32 = jnp.float32
BF16 = jnp.bfloat16

D_MODEL = 4096
CONV_WIDTH = D_MODEL // 2
CONV_K = 31
HEAD_DIM = 128
N_HEADS = (D_MODEL // 2) // HEAD_DIM
N_KV_HEADS = N_HEADS // 4
GROUP = N_HEADS // N_KV_HEADS
ROT_DIM = HEAD_DIM // 4
ROPE_THETA = 500000.0
WINDOW = 128
BLOCK = 128
D_FF = 11008
FFN_CONV_K = 3
EPS = 1e-6
PAST_LEN = 8192
Q_WIDTH = N_HEADS * HEAD_DIM
KV_WIDTH = N_KV_HEADS * HEAD_DIM
A_END = 2 * CONV_WIDTH
V_END = A_END + Q_WIDTH + 2 * KV_WIDTH
QKV_WIDTH = Q_WIDTH + 2 * KV_WIDTH
SCALE = HEAD_DIM ** -0.5

V7X_VMEM_BYTES = 64 * 1024 * 1024
VMEM_LIMIT = V7X_VMEM_BYTES - 8 * 1024 * 1024
SUBLANES = 8
LANES = 128
CONV_HALO = 32
ROW_CHUNK = 128


def _params(n_axes):
    return pltpu.CompilerParams(dimension_semantics=("arbitrary",) * n_axes,
                                vmem_limit_bytes=VMEM_LIMIT)


def _sigmoid(x):
    return 1.0 / (1.0 + jnp.exp(-x))


def _row_chunks(tm):
    rc = min(tm, ROW_CHUNK)
    return [slice(r * rc, (r + 1) * rc) for r in range(tm // rc)]


def _rmsnorm_body(x_ref, g_ref, o_ref):
    x = x_ref[...]
    ms = jnp.mean(x * x, axis=-1, keepdims=True)
    o_ref[...] = (x * lax.rsqrt(ms + EPS) * g_ref[...]).astype(o_ref.dtype)


def _rmsnorm(x, g, out_dtype, tr):
    m, d = x.shape
    tr = min(tr, m)
    return pl.pallas_call(
        _rmsnorm_body,
        grid=(m // tr,),
        in_specs=[pl.BlockSpec((tr, d), lambda i: (i, 0)),
                  pl.BlockSpec((1, d), lambda i: (0, 0))],
        out_specs=pl.BlockSpec((tr, d), lambda i: (i, 0)),
        out_shape=jax.ShapeDtypeStruct((m, d), out_dtype),
        compiler_params=_params(1),
        name="rmsnorm",
    )(x, g.reshape(1, d))


def _rope_table_body(inv_ref, sign_ref, cos_ref, sin_ref, *, pos0, pos_step):
    rows = cos_ref.shape[0]
    i = pl.program_id(0)
    r = lax.broadcasted_iota(jnp.int32, (rows, HEAD_DIM), 0) + i * rows
    lane = lax.broadcasted_iota(jnp.int32, (rows, HEAD_DIM), 1)
    pos = (pos0 + pos_step * r).astype(F32)
    ang = pos * inv_ref[...]
    cos_ref[...] = jnp.where(lane < ROT_DIM, jnp.cos(ang), 1.0)
    sin_ref[...] = sign_ref[...] * jnp.sin(ang)


def _rope_tables(n_rows, pos0, pos_step):
    half = ROT_DIM // 2
    inv = ROPE_THETA ** (-2.0 * jnp.arange(half, dtype=F32) / ROT_DIM)
    zeros = jnp.zeros((HEAD_DIM - ROT_DIM,), F32)
    inv_full = jnp.concatenate([inv, inv, zeros]).reshape(1, HEAD_DIM)
    sign = jnp.concatenate([-jnp.ones((half,), F32), jnp.ones((half,), F32), zeros]).reshape(1, HEAD_DIM)
    tr = min(n_rows, 512)
    spec_c = pl.BlockSpec((1, HEAD_DIM), lambda i: (0, 0))
    spec_t = pl.BlockSpec((tr, HEAD_DIM), lambda i: (i, 0))
    return pl.pallas_call(
        functools.partial(_rope_table_body, pos0=pos0, pos_step=pos_step),
        grid=(n_rows // tr,),
        in_specs=[spec_c, spec_c],
        out_specs=[spec_t, spec_t],
        out_shape=[jax.ShapeDtypeStruct((n_rows, HEAD_DIM), F32)] * 2,
        compiler_params=_params(1),
        name="rope_tables",
    )(inv_full, sign)


def _weight_copy(w_hbm, fetch_ref, sem, row0, col0):
    ks, tn = fetch_ref.shape
    return pltpu.make_async_copy(w_hbm.at[pl.ds(row0, ks), pl.ds(col0, tn)], fetch_ref, sem)


def _advance_weights(j, n_col_blocks, w_refs, geoms, fetch_refs, wbf_refs, sems):
    def copies(col_block):
        out = []
        for d, (w_hbm, fetch_ref, (row0, col_block0)) in enumerate(zip(w_refs, fetch_refs, geoms)):
            col0 = pl.multiple_of((col_block0 + col_block) * fetch_ref.shape[1], LANES)
            out.append(_weight_copy(w_hbm, fetch_ref, sems.at[d], row0, col0))
        return out

    @pl.when(j == 0)
    def _fetch_first():
        for c in copies(0):
            c.start()

    for c in copies(j):
        c.wait()
    for fetch_ref, s_ref in zip(fetch_refs, wbf_refs):
        s_ref[...] = fetch_ref[...].astype(BF16)

    @pl.when(j + 1 < n_col_blocks)
    def _fetch_next():
        for c in copies(j + 1):
            c.start()


def _ws_body(*refs, dot_lhs, w_geoms, n_lhs, n_extra, n_out, tms, epilogue):
    n_dots = len(dot_lhs)
    pos = 0
    def take(n):
        nonlocal pos
        out = refs[pos:pos + n]
        pos += n
        return out
    lhs_p, lhs_s = take(n_lhs), take(n_lhs)
    w_refs = take(n_dots)
    ex_p, ex_s = take(n_extra), take(n_extra)
    out_p, out_s = take(n_out), take(n_out)
    fetch_refs = take(n_dots)
    wbf_refs = take(n_dots)
    stage_bufs = (take(n_dots), take(n_dots))
    (sems,) = take(1)
    j = pl.program_id(0)
    i = pl.program_id(1)

    def run_rows(lhs_refs, ex_refs, out_refs, tm):
        chunks = _row_chunks(tm)
        n = chunks[0].stop
        for s in range(len(chunks) + 1):
            if s < len(chunks):
                for li, s_ref, r_ref in zip(dot_lhs, wbf_refs, stage_bufs[s % 2]):
                    r_ref[0:n, :] = jnp.dot(lhs_refs[li][chunks[s], :], s_ref[...], preferred_element_type=F32)
            if s >= 1:
                rows = chunks[s - 1]
                accs = [r_ref[0:n, :] for r_ref in stage_bufs[(s - 1) % 2]]
                for o_ref, val in zip(out_refs, epilogue(accs, ex_refs, rows, j)):
                    o_ref[rows, :] = val.astype(o_ref.dtype)

    @pl.when(i == 0)
    def _new_column_block():
        _advance_weights(j, pl.num_programs(0), w_refs, w_geoms, fetch_refs, wbf_refs, sems)
        run_rows(lhs_s, ex_s, out_s, tms[1])

    run_rows(lhs_p, ex_p, out_p, tms[0])


def _ws_matmul(lhs, dots, extras, epilogue, out_dtypes, *, n_cols, tm, tn, name):
    m_p = lhs[0][0][0].shape[0]
    m_s = lhs[0][0][1].shape[0]
    in_specs, args = [], []
    for grp in (0, 1):
        for arrs, kb, ks in lhs:
            if grp == 0:
                in_specs.append(pl.BlockSpec((tm, ks), lambda j, i, kb=kb: (i, kb)))
            else:
                in_specs.append(pl.BlockSpec((m_s, ks), lambda j, i, kb=kb: (0, kb)))
            args.append(arrs[grp])
    w_geoms = []
    for li, w, off in dots:
        _, kb, ks = lhs[li]
        in_specs.append(pl.BlockSpec(memory_space=pl.ANY))
        args.append(w)
        w_geoms.append((kb * ks, off // tn))
    for grp in (0, 1):
        for arrs, cols, imap in extras:
            if grp == 0:
                in_specs.append(pl.BlockSpec((tm, cols), imap))
            else:
                in_specs.append(pl.BlockSpec((m_s, cols), (lambda j, i: (0, j)) if cols == tn else (lambda j, i: (0, 0))))
            args.append(arrs[grp])
    out_specs = ([pl.BlockSpec((tm, tn), lambda j, i: (i, j)) for _ in out_dtypes]
                 + [pl.BlockSpec((m_s, tn), lambda j, i: (0, j)) for _ in out_dtypes])
    out_shape = ([jax.ShapeDtypeStruct((m_p, n_cols), dt) for dt in out_dtypes]
                 + [jax.ShapeDtypeStruct((m_s, n_cols), dt) for dt in out_dtypes])
    body = functools.partial(_ws_body, dot_lhs=tuple(l for l, _, _ in dots), w_geoms=tuple(w_geoms),
                             n_lhs=len(lhs), n_extra=len(extras), n_out=len(out_dtypes), tms=(tm, m_s),
                             epilogue=epilogue)
    outs = pl.pallas_call(
        body,
        grid=(n_cols // tn, m_p // tm),
        in_specs=in_specs,
        out_specs=out_specs,
        out_shape=out_shape,
        scratch_shapes=([pltpu.VMEM((lhs[li][2], tn), F32) for li, _, _ in dots]
                        + [pltpu.VMEM((lhs[li][2], tn), BF16) for li, _, _ in dots]
                        + [pltpu.VMEM((min(tm, ROW_CHUNK), tn), F32) for _ in range(2 * len(dots))]
                        + [pltpu.SemaphoreType.DMA((len(dots),))]),
        compiler_params=_params(2),
        name=name,
    )(*args)
    n = len(out_dtypes)
    return list(zip(outs[:n], outs[n:]))


def _glu_epilogue(accs, ex_refs, rows, j):
    return [accs[0] * _sigmoid(accs[1])]


def _rope_epilogue(accs, ex_refs, rows, j, *, tn):
    cos_ref, sin_ref = ex_refs
    z = accs[0]
    rotated = j < (Q_WIDTH + KV_WIDTH) // tn
    cos_t = jnp.where(rotated, cos_ref[rows, :], 1.0)
    sin_t = jnp.where(rotated, sin_ref[rows, :], 0.0)
    lane = lax.broadcasted_iota(jnp.int32, cos_t.shape, 1)
    half = ROT_DIM // 2
    outs = []
    for c in range(tn // HEAD_DIM):
        x = z[:, c * HEAD_DIM:(c + 1) * HEAD_DIM]
        partner = jnp.where(lane < half, pltpu.roll(x, HEAD_DIM - half, 1), pltpu.roll(x, half, 1))
        outs.append(x * cos_t + partner * sin_t)
    return [jnp.concatenate(outs, axis=1)]


def _merge_epilogue(accs, ex_refs, rows, j):
    gate_a, br_a, gate_b, br_b = accs
    return [_sigmoid(gate_a) * br_a + _sigmoid(gate_b) * br_b]


def _residual_epilogue(accs, ex_refs, rows, j):
    return [ex_refs[0][rows, :] + accs[0]]


def _ln_swish(y, g_ref, b_ref):
    mu = jnp.mean(y, axis=-1, keepdims=True)
    yc = y - mu
    var = jnp.mean(yc * yc, axis=-1, keepdims=True)
    t = yc * lax.rsqrt(var + EPS) * g_ref[...] + b_ref[...]
    return t * _sigmoid(t)


def _conv_a_prompt_body(cur_ref, prev_ref, w_ref, b_ref, g_ref, beta_ref, o_ref, xx_ref, sh_ref, y_ref, *, tl):
    first = pl.program_id(1) == 0
    xx_ref[0:CONV_HALO, :] = jnp.where(first, 0.0, prev_ref[...])
    xx_ref[CONV_HALO:, :] = cur_ref[...]
    n_sh = sh_ref.shape[1]
    for b in range(1, SUBLANES):
        sh_ref[b - 1] = xx_ref[b:b + n_sh, :]
    rows, lanes = 32, 512
    base = CONV_HALO - (CONV_K - 1)

    def row_chunk(rc, carry):
        r0 = pl.multiple_of(rc * rows, rows)
        for cc in range(CONV_WIDTH // lanes):
            cs = slice(cc * lanes, (cc + 1) * lanes)
            acc = jnp.zeros((rows, lanes), F32) + b_ref[:, cs]
            for k in range(CONV_K):
                a, b = divmod(base + k, SUBLANES)
                start = r0 + a * SUBLANES
                if b == 0:
                    x = xx_ref[pl.ds(start, rows), cs]
                else:
                    x = sh_ref[b - 1, pl.ds(start, rows), cs]
                acc = acc + w_ref[k:k + 1, cs] * x
            y_ref[pl.ds(r0, rows), cs] = acc
        return carry

    lax.fori_loop(0, tl // rows, row_chunk, 0)
    o_ref[...] = _ln_swish(y_ref[...], g_ref, beta_ref).astype(o_ref.dtype)


def _conv_a_prompt(u, conv_w, conv_b, ln_g, ln_b, *, n_batch, seq, tl):
    c = CONV_WIDTH
    nl = seq // tl
    halo_per_tile = tl // CONV_HALO
    row = lambda a: a.reshape(1, c)
    vec = pl.BlockSpec((1, c), lambda b, l: (0, 0))
    return pl.pallas_call(
        functools.partial(_conv_a_prompt_body, tl=tl),
        grid=(n_batch, nl),
        in_specs=[pl.BlockSpec((tl, c), lambda b, l: (b * nl + l, 0)),
                  pl.BlockSpec((CONV_HALO, c),
                               lambda b, l: (jnp.maximum((b * nl + l) * halo_per_tile - 1, 0), 0)),
                  pl.BlockSpec((CONV_K, c), lambda b, l: (0, 0)),
                  vec, vec, vec],
        out_specs=pl.BlockSpec((tl, c), lambda b, l: (b * nl + l, 0)),
        out_shape=jax.ShapeDtypeStruct((n_batch * seq, c), BF16),
        scratch_shapes=[pltpu.VMEM((tl + CONV_HALO, c), F32),
                        pltpu.VMEM((SUBLANES - 1, tl + CONV_HALO - SUBLANES, c), F32),
                        pltpu.VMEM((tl, c), F32)],
        compiler_params=_params(2),
        name="conv_a_prompt",
    )(u, u, conv_w, row(conv_b), row(ln_g), row(ln_b))


def _conv_a_sample_body(hist_ref, u_ref, w_ref, b_ref, g_ref, beta_ref, o_ref, new_ref):
    u = u_ref[...]
    y = u * w_ref[CONV_K - 1:CONV_K, :] + b_ref[...]
    for k in range(CONV_K - 1):
        y = y + hist_ref[k] * w_ref[k:k + 1, :]
    o_ref[...] = _ln_swish(y, g_ref, beta_ref).astype(o_ref.dtype)
    new_ref[0:CONV_K - 2] = hist_ref[1:CONV_K - 1]
    new_ref[CONV_K - 2] = u


def _conv_a_sample(hist_t, u, conv_w, conv_b, ln_g, ln_b, *, bs):
    n, c = u.shape
    row = lambda a: a.reshape(1, c)
    vec = pl.BlockSpec((1, c), lambda b: (0, 0))
    hist_spec = pl.BlockSpec((CONV_K - 1, bs, c), lambda b: (0, b, 0))
    return pl.pallas_call(
        _conv_a_sample_body,
        grid=(n // bs,),
        in_specs=[hist_spec,
                  pl.BlockSpec((bs, c), lambda b: (b, 0)),
                  pl.BlockSpec((CONV_K, c), lambda b: (0, 0)),
                  vec, vec, vec],
        out_specs=[pl.BlockSpec((bs, c), lambda b: (b, 0)), hist_spec],
        out_shape=[jax.ShapeDtypeStruct((n, c), BF16), jax.ShapeDtypeStruct(hist_t.shape, F32)],
        compiler_params=_params(1),
        name="conv_a_sample",
    )(hist_t, u, conv_w, row(conv_b), row(ln_g), row(ln_b))


def _softmax_with_sink(s, sink):
    m = jnp.maximum(jnp.max(s, axis=-1, keepdims=True), sink)
    p = jnp.exp(s - m)
    return p / (jnp.sum(p, axis=-1, keepdims=True) + jnp.exp(sink - m))


def _attn_prompt_body(q_ref, kc_ref, kp_ref, vc_ref, vp_ref, sink_ref, o_ref):
    n = pl.program_id(1)
    rows = GROUP * BLOCK
    r = lax.broadcasted_iota(jnp.int32, (rows, 2 * BLOCK), 0) % BLOCK
    c = lax.broadcasted_iota(jnp.int32, (rows, 2 * BLOCK), 1)
    diff = (r + BLOCK) - c
    mask = (diff >= 0) & (diff < WINDOW) & ((c >= BLOCK) | (n > 0))
    for kh in range(N_KV_HEADS):
        ks = slice(kh * HEAD_DIM, (kh + 1) * HEAD_DIM)
        kk = jnp.concatenate([kp_ref[:, ks], kc_ref[:, ks]], axis=0).astype(BF16)
        vv = jnp.concatenate([vp_ref[:, ks], vc_ref[:, ks]], axis=0).astype(BF16)
        heads = [kh * GROUP + g for g in range(GROUP)]
        qg = jnp.concatenate([q_ref[:, h * HEAD_DIM:(h + 1) * HEAD_DIM] for h in heads],
                             axis=0).astype(BF16)
        sink = jnp.concatenate([jnp.broadcast_to(sink_ref[h:h + 1, 0:1], (BLOCK, 1)) for h in heads],
                               axis=0)
        s = lax.dot_general(qg, kk, (((1,), (1,)), ((), ())), preferred_element_type=F32) * SCALE
        p = _softmax_with_sink(jnp.where(mask, s, -jnp.inf), sink)
        o = jnp.dot(p.astype(BF16), vv, preferred_element_type=F32)
        for g, h in enumerate(heads):
            o_ref[:, h * HEAD_DIM:(h + 1) * HEAD_DIM] = o[g * BLOCK:(g + 1) * BLOCK].astype(o_ref.dtype)


def _attn_prompt(qkv, sinks, *, n_batch, seq):
    nb = seq // BLOCK
    k_col = Q_WIDTH // KV_WIDTH
    cur = lambda col: (lambda b, n: (b * nb + n, col))
    prev = lambda col: (lambda b, n: (jnp.maximum(b * nb + n - 1, 0), col))
    kv = lambda imap: pl.BlockSpec((BLOCK, KV_WIDTH), imap)
    return pl.pallas_call(
        _attn_prompt_body,
        grid=(n_batch, nb),
        in_specs=[pl.BlockSpec((BLOCK, Q_WIDTH), cur(0)),
                  kv(cur(k_col)), kv(prev(k_col)), kv(cur(k_col + 1)), kv(prev(k_col + 1)),
                  pl.BlockSpec((N_HEADS, HEAD_DIM), lambda b, n: (0, 0))],
        out_specs=pl.BlockSpec((BLOCK, Q_WIDTH), cur(0)),
        out_shape=jax.ShapeDtypeStruct((n_batch * seq, Q_WIDTH), BF16),
        compiler_params=_params(2),
        name="attn_prompt",
    )(qkv, qkv, qkv, qkv, qkv, sinks)


def _attn_sample_body(qkv_ref, ck_ref, cv_ref, sink_ref, o_ref, ko_ref, vo_ref, *, bs):
    n_rows = WINDOW * N_KV_HEADS
    keep = n_rows - N_KV_HEADS
    k_row = N_HEADS
    v_row = N_HEADS + N_KV_HEADS
    head = lax.broadcasted_iota(jnp.int32, (N_HEADS, n_rows), 0)
    col = lax.broadcasted_iota(jnp.int32, (N_HEADS, n_rows), 1)
    same_group = (col % N_KV_HEADS) == (head // GROUP)
    sink = sink_ref[:, 0:1]

    def per_sample(b, carry):
        ko_ref[b, 0:keep, :] = ck_ref[b, N_KV_HEADS:n_rows, :]
        ko_ref[b, keep:n_rows, :] = qkv_ref[b, k_row:k_row + N_KV_HEADS, :]
        vo_ref[b, 0:keep, :] = cv_ref[b, N_KV_HEADS:n_rows, :]
        vo_ref[b, keep:n_rows, :] = qkv_ref[b, v_row:v_row + N_KV_HEADS, :]
        q = qkv_ref[b, 0:N_HEADS, :].astype(BF16)
        kk = ko_ref[b].astype(BF16)
        vv = vo_ref[b].astype(BF16)
        s = lax.dot_general(q, kk, (((1,), (1,)), ((), ())), preferred_element_type=F32) * SCALE
        p = _softmax_with_sink(jnp.where(same_group, s, -jnp.inf), sink)
        o_ref[b] = jnp.dot(p.astype(BF16), vv, preferred_element_type=F32).astype(o_ref.dtype)
        return carry

    lax.fori_loop(0, bs, per_sample, 0)


def _attn_sample(qkv, cache_k, cache_v, sinks, *, bs):
    n = qkv.shape[0]
    n_rows = QKV_WIDTH // HEAD_DIM
    cache = pl.BlockSpec((bs, WINDOW * N_KV_HEADS, HEAD_DIM), lambda b: (b, 0, 0))
    o, k_new, v_new = pl.pallas_call(
        functools.partial(_attn_sample_body, bs=bs),
        grid=(n // bs,),
        in_specs=[pl.BlockSpec((bs, n_rows, HEAD_DIM), lambda b: (b, 0, 0)), cache, cache,
                  pl.BlockSpec((N_HEADS, HEAD_DIM), lambda b: (0, 0))],
        out_specs=[pl.BlockSpec((bs, N_HEADS, HEAD_DIM), lambda b: (b, 0, 0)), cache, cache],
        out_shape=[jax.ShapeDtypeStruct((n, N_HEADS, HEAD_DIM), BF16),
                   jax.ShapeDtypeStruct(cache_k.shape, F32), jax.ShapeDtypeStruct(cache_v.shape, F32)],
        compiler_params=_params(1),
        name="attn_sample",
    )(qkv.reshape(n, n_rows, HEAD_DIM), cache_k, cache_v, sinks)
    return o.reshape(n, Q_WIDTH), k_new, v_new


def _ffn_body(xp_ref, xs_ref, wg_ref, wv_ref, cwg_ref, cwv_ref, cbg_ref, cbv_ref,
              h0g_ref, h0v_ref, h1g_ref, h1v_ref,
              fp_ref, tg_ref, tv_ref, fs_ref, rg_ref, rv_ref,
              wg_fetch, wv_fetch, wg_bf, wv_bf, eg0_ref, eg1_ref, ev0_ref, ev1_ref, sems,
              *, tm, tiles_per_seq, up_col_block0):
    i = pl.program_id(1)
    chunks = _row_chunks(tm)
    rc = chunks[0].stop
    assert len(chunks) % 2 == 0
    stage_bufs = ((eg0_ref, ev0_ref), (eg1_ref, ev1_ref))
    last_bufs = stage_bufs[(len(chunks) - 1) % 2]
    head = slice(0, SUBLANES)
    tail = slice(rc, rc + SUBLANES)

    @pl.when(i == 0)
    def _new_column_block():
        _advance_weights(pl.program_id(0), pl.num_programs(0), (wg_ref, wv_ref), ((0, 0), (0, up_col_block0)),
                         (wg_fetch, wv_fetch), (wg_bf, wv_bf), sems)
        x = xs_ref[...]
        ys = []
        for w_bf, cw_ref, cb_ref, h0_ref, h1_ref, r_ref in ((wg_bf, cwg_ref, cbg_ref, h0g_ref, h1g_ref, rg_ref),
                                                            (wv_bf, cwv_ref, cbv_ref, h0v_ref, h1v_ref, rv_ref)):
            d = jnp.dot(x, w_bf[...], preferred_element_type=F32)
            r_ref[...] = d
            ys.append(cw_ref[0:1, :] * h0_ref[...] + cw_ref[1:2, :] * h1_ref[...] + cw_ref[2:3, :] * d
                      + cb_ref[...])
        fs_ref[...] = (ys[0] * _sigmoid(ys[0]) * ys[1]).astype(fs_ref.dtype)

    @pl.when(i % tiles_per_seq == 0)
    def _zero_history():
        for e_ref in last_bufs:
            e_ref[tail, :] = jnp.zeros((SUBLANES, e_ref.shape[1]), F32)

    def conv(e_ref, cw_ref, cb_ref):
        y = cb_ref[...]
        for k in range(FFN_CONV_K):
            r0 = SUBLANES - (FFN_CONV_K - 1) + k
            y = y + cw_ref[k:k + 1, :] * e_ref[r0:r0 + rc, :]
        return y

    for s in range(len(chunks) + 1):
        if s < len(chunks):
            x = xp_ref[chunks[s], :]
            for e_ref, e_prev, w_bf in zip(stage_bufs[s % 2], stage_bufs[(s - 1) % 2], (wg_bf, wv_bf)):
                e_ref[head, :] = e_prev[tail, :]
                e_ref[SUBLANES:, :] = jnp.dot(x, w_bf[...], preferred_element_type=F32)
        if s >= 1:
            eg_ref, ev_ref = stage_bufs[(s - 1) % 2]
            yg = conv(eg_ref, cwg_ref, cbg_ref)
            yv = conv(ev_ref, cwv_ref, cbv_ref)
            fp_ref[chunks[s - 1], :] = (yg * _sigmoid(yg) * yv).astype(fp_ref.dtype)
    for e_ref, t_ref in zip(last_bufs, (tg_ref, tv_ref)):
        t_ref[0] = e_ref[tail, :]


def _ffn_up(x_p, x_s, w_up, conv_w, conv_b, hist, *, tm, tn, seq):
    m, d = x_p.shape
    m_s = x_s.shape[0]
    half_blocks = D_FF // tn
    tiles_per_seq = seq // tm
    col = lambda off: (lambda j, i: (0, off + j))
    conv_b = conv_b.reshape(1, 2 * D_FF)
    in_specs = [pl.BlockSpec((tm, d), lambda j, i: (i, 0)),
                pl.BlockSpec((m_s, d), lambda j, i: (0, 0)),
                pl.BlockSpec(memory_space=pl.ANY), pl.BlockSpec(memory_space=pl.ANY),
                pl.BlockSpec((FFN_CONV_K, tn), col(0)), pl.BlockSpec((FFN_CONV_K, tn), col(half_blocks)),
                pl.BlockSpec((1, tn), col(0)), pl.BlockSpec((1, tn), col(half_blocks))]
    in_specs += [pl.BlockSpec((m_s, tn), col(k * half_blocks)) for k in range(4)]
    args = [x_p, x_s, w_up, w_up, conv_w, conv_w, conv_b, conv_b] + [hist] * 4
    tail_spec = pl.BlockSpec((1, SUBLANES, tn), lambda j, i: (i // tiles_per_seq, 0, j))
    tail_shape = jax.ShapeDtypeStruct((m // seq, SUBLANES, D_FF), F32)
    s_spec = pl.BlockSpec((m_s, tn), lambda j, i: (0, j))
    return pl.pallas_call(
        functools.partial(_ffn_body, tm=tm, tiles_per_seq=tiles_per_seq, up_col_block0=half_blocks),
        grid=(half_blocks, m // tm),
        in_specs=in_specs,
        out_specs=[pl.BlockSpec((tm, tn), lambda j, i: (i, j)), tail_spec, tail_spec, s_spec, s_spec, s_spec],
        out_shape=[jax.ShapeDtypeStruct((m, D_FF), BF16), tail_shape, tail_shape,
                   jax.ShapeDtypeStruct((m_s, D_FF), BF16),
                   jax.ShapeDtypeStruct((m_s, D_FF), F32), jax.ShapeDtypeStruct((m_s, D_FF), F32)],
        scratch_shapes=([pltpu.VMEM((d, tn), F32)] * 2 + [pltpu.VMEM((d, tn), BF16)] * 2
                        + [pltpu.VMEM((min(tm, ROW_CHUNK) + SUBLANES, tn), F32)] * 4
                        + [pltpu.SemaphoreType.DMA((2,))]),
        compiler_params=_params(2),
        name="ffn_up",
    )(*args)


def kernel(x_prompt, x_sample, cache_k_win, cache_v_win, state_conv_a, state_conv_ffn, norm1_g, w_in,
           conv_a_w, conv_a_b, ln_a_g, ln_a_b, w_a_out, attn_sinks, w_attn_out, w_o, norm2_g, w_up,
           conv_f_w, conv_f_b, w_down, norm_f_g):
    n_batch, seq, d = x_prompt.shape
    n_dec = x_sample.shape[0]
    assert x_sample.shape[1] == 1 and w_in.shape[0] == 1
    w_in, w_a_out, w_attn_out, w_o, w_down = w_in[0], w_a_out[0], w_attn_out[0], w_o[0], w_down[0]
    sinks = jnp.broadcast_to(attn_sinks[0][:, None], (N_HEADS, HEAD_DIM))
    conv_a_args = (conv_a_w[0], conv_a_b[0], ln_a_g[0], ln_a_b[0])
    tm, tr = 1024, 256
    tile = lambda j, i: (i, j)
    full = lambda arrs: (arrs, 0, arrs[0].shape[1])

    x = (x_prompt.reshape(n_batch * seq, d), x_sample.reshape(n_dec, d))
    xn = tuple(_rmsnorm(a, norm1_g[0], BF16, tr) for a in x)
    (u,) = _ws_matmul([full(xn)], [(0, w_in, 0), (0, w_in, CONV_WIDTH)], [], _glu_epilogue, [F32],
                      n_cols=CONV_WIDTH, tm=tm, tn=512, name="in_glu")
    cos_p, sin_p = _rope_tables(seq, 0, 1)
    cos_s, sin_s = _rope_tables(n_dec, PAST_LEN, 0)
    rope_map = lambda j, i: (i % (seq // tm), 0)
    tn_qkv = KV_WIDTH
    (qkv,) = _ws_matmul([full(xn)], [(0, w_in, A_END)],
                        [((cos_p, cos_s), HEAD_DIM, rope_map), ((sin_p, sin_s), HEAD_DIM, rope_map)],
                        functools.partial(_rope_epilogue, tn=tn_qkv), [F32],
                        n_cols=QKV_WIDTH, tm=tm, tn=tn_qkv, name="in_qkv_rope")

    c_p = _conv_a_prompt(u[0], *conv_a_args, n_batch=n_batch, seq=seq, tl=256)
    hist_a = jnp.transpose(state_conv_a[0], (1, 0, 2))
    c_s, hist_a_new = _conv_a_sample(hist_a, u[1], *conv_a_args, bs=8)

    o_p = _attn_prompt(qkv[0], sinks, n_batch=n_batch, seq=seq)
    cache_shape = (n_dec, WINDOW * N_KV_HEADS, HEAD_DIM)
    o_s, k_win_s, v_win_s = _attn_sample(qkv[1], cache_k_win[0].reshape(cache_shape),
                                         cache_v_win[0].reshape(cache_shape), sinks, bs=16)

    (merged,) = _ws_matmul([full(xn), full((c_p, c_s)), full((o_p, o_s))],
                           [(0, w_in, V_END), (1, w_a_out, 0), (0, w_in, V_END + D_MODEL), (2, w_attn_out, 0)],
                           [], _merge_epilogue, [BF16],
                           n_cols=D_MODEL, tm=256, tn=512, name="gated_merge")
    (h,) = _ws_matmul([full(merged)], [(0, w_o, 0)], [(x, 512, tile)], _residual_epilogue, [F32],
                      n_cols=D_MODEL, tm=tm, tn=512, name="out_proj")

    hn = tuple(_rmsnorm(a, norm2_g[0], BF16, tr) for a in h)
    ffn_hist = state_conv_ffn[0].reshape(n_dec, (FFN_CONV_K - 1) * 2 * D_FF)
    f_p, tail_g, tail_v, f_s, raw_g, raw_v = _ffn_up(hn[0], hn[1], w_up[0], conv_f_w[0], conv_f_b[0], ffn_hist,
                                                     tm=2 * tm, tn=256, seq=seq)
    y = h
    k_half = D_FF // 2
    for kb in range(2):
        (y,) = _ws_matmul([((f_p, f_s), kb, k_half)], [(0, w_down, 0)], [(y, 512, tile)], _residual_epilogue,
                          [F32], n_cols=D_MODEL, tm=tm, tn=512, name="ffn_down")
    y_p, y_s = (_rmsnorm(a, norm_f_g, F32, tr) for a in y)

    qkv_p = qkv[0].reshape(n_batch, seq, QKV_WIDTH)
    win_shape = (1, n_batch, WINDOW, N_KV_HEADS, HEAD_DIM)
    k_win_p = qkv_p[:, seq - WINDOW:, Q_WIDTH:Q_WIDTH + KV_WIDTH].reshape(win_shape)
    v_win_p = qkv_p[:, seq - WINDOW:, Q_WIDTH + KV_WIDTH:].reshape(win_shape)
    conv_a_p = u[0].reshape(n_batch, seq, CONV_WIDTH)[None, :, seq - (CONV_K - 1):]
    keep = SUBLANES - (FFN_CONV_K - 1)
    conv_ffn_p = jnp.concatenate([tail_g[:, keep:], tail_v[:, keep:]], axis=-1)[None]
    conv_a_s = jnp.transpose(hist_a_new, (1, 0, 2))[None]
    up_new = jnp.concatenate([raw_g, raw_v], axis=-1)[:, None]
    conv_ffn_s = jnp.concatenate([state_conv_ffn[0][:, 1:], up_new], axis=1)[None]
    return (y_p.reshape(n_batch, seq, d), y_s.reshape(n_dec, 1, d), k_win_p, v_win_p, conv_a_p, conv_ffn_p,
            k_win_s.reshape(cache_k_win.shape), v_win_s.reshape(cache_v_win.shape), conv_a_s, conv_ffn_s)
```

```python
import functools

import jax
import jax.numpy as jnp
from jax import lax
from jax.experimental import pallas as pl
from jax.experimental.pallas import tpu as pltpu

F32 = jnp.float32
BF16 = jnp.bfloat16

D_MODEL = 4096
CONV_WIDTH = D_MODEL // 2
CONV_K = 31
HEAD_DIM = 128
N_HEADS = (D_MODEL // 2) // HEAD_DIM
N_KV_HEADS = N_HEADS // 4
GROUP = N_HEADS // N_KV_HEADS
ROT_DIM = HEAD_DIM // 4
ROPE_THETA = 500000.0
WINDOW = 128
BLOCK = 128
D_FF = 11008
FFN_CONV_K = 3
EPS = 1e-6
PAST_LEN = 8192
Q_WIDTH = N_HEADS * HEAD_DIM
KV_WIDTH = N_KV_HEADS * HEAD_DIM
A_END = 2 * CONV_WIDTH
V_END = A_END + Q_WIDTH + 2 * KV_WIDTH
QKV_WIDTH = Q_WIDTH + 2 * KV_WIDTH
SCALE = HEAD_DIM ** -0.5

V7X_VMEM_BYTES = 64 * 1024 * 1024
VMEM_LIMIT = V7X_VMEM_BYTES - 8 * 1024 * 1024
SUBLANES = 8
LANES = 128
CONV_HALO = 32
ROW_CHUNK = 128


def _params(n_axes):
    return pltpu.CompilerParams(dimension_semantics=("arbitrary",) * n_axes,
                                vmem_limit_bytes=VMEM_LIMIT)


def _sigmoid(x):
    return 1.0 / (1.0 + jnp.exp(-x))


def _row_chunks(tm):
    rc = min(tm, ROW_CHUNK)
    return [slice(r * rc, (r + 1) * rc) for r in range(tm // rc)]


def _rmsnorm_body(x_ref, g_ref, o_ref):
    x = x_ref[...]
    ms = jnp.mean(x * x, axis=-1, keepdims=True)
    o_ref[...] = (x * lax.rsqrt(ms + EPS) * g_ref[...]).astype(o_ref.dtype)


def _rmsnorm(x, g, out_dtype, tr):
    m, d = x.shape
    tr = min(tr, m)
    return pl.pallas_call(
        _rmsnorm_body,
        grid=(m // tr,),
        in_specs=[pl.BlockSpec((tr, d), lambda i: (i, 0)),
                  pl.BlockSpec((1, d), lambda i: (0, 0))],
        out_specs=pl.BlockSpec((tr, d), lambda i: (i, 0)),
        out_shape=jax.ShapeDtypeStruct((m, d), out_dtype),
        compiler_params=_params(1),
        name="rmsnorm",
    )(x, g.reshape(1, d))


def _rope_table_body(inv_ref, sign_ref, cos_ref, sin_ref, *, pos0, pos_step):
    rows = cos_ref.shape[0]
    i = pl.program_id(0)
    r = lax.broadcasted_iota(jnp.int32, (rows, HEAD_DIM), 0) + i * rows
    lane = lax.broadcasted_iota(jnp.int32, (rows, HEAD_DIM), 1)
    pos = (pos0 + pos_step * r).astype(F32)
    ang = pos * inv_ref[...]
    cos_ref[...] = jnp.where(lane < ROT_DIM, jnp.cos(ang), 1.0)
    sin_ref[...] = sign_ref[...] * jnp.sin(ang)


def _rope_tables(n_rows, pos0, pos_step):
    half = ROT_DIM // 2
    inv = ROPE_THETA ** (-2.0 * jnp.arange(half, dtype=F32) / ROT_DIM)
    zeros = jnp.zeros((HEAD_DIM - ROT_DIM,), F32)
    inv_full = jnp.concatenate([inv, inv, zeros]).reshape(1, HEAD_DIM)
    sign = jnp.concatenate([-jnp.ones((half,), F32), jnp.ones((half,), F32), zeros]).reshape(1, HEAD_DIM)
    tr = min(n_rows, 512)
    spec_c = pl.BlockSpec((1, HEAD_DIM), lambda i: (0, 0))
    spec_t = pl.BlockSpec((tr, HEAD_DIM), lambda i: (i, 0))
    return pl.pallas_call(
        functools.partial(_rope_table_body, pos0=pos0, pos_step=pos_step),
        grid=(n_rows // tr,),
        in_specs=[spec_c, spec_c],
        out_specs=[spec_t, spec_t],
        out_shape=[jax.ShapeDtypeStruct((n_rows, HEAD_DIM), F32)] * 2,
        compiler_params=_params(1),
        name="rope_tables",
    )(inv_full, sign)


def _weight_copy(w_hbm, fetch_ref, sem, row0, col0):
    ks, tn = fetch_ref.shape
    return pltpu.make_async_copy(w_hbm.at[pl.ds(row0, ks), pl.ds(col0, tn)], fetch_ref, sem)


def _advance_weights(j, n_col_blocks, w_refs, geoms, fetch_refs, wbf_refs, sems):
    def copies(col_block):
        out = []
        for d, (w_hbm, fetch_ref, (row0, col_block0)) in enumerate(zip(w_refs, fetch_refs, geoms)):
            col0 = pl.multiple_of((col_block0 + col_block) * fetch_ref.shape[1], LANES)
            out.append(_weight_copy(w_hbm, fetch_ref, sems.at[d], row0, col0))
        return out

    @pl.when(j == 0)
    def _fetch_first():
        for c in copies(0):
            c.start()

    for c in copies(j):
        c.wait()
    for fetch_ref, s_ref in zip(fetch_refs, wbf_refs):
        s_ref[...] = fetch_ref[...].astype(BF16)

    @pl.when(j + 1 < n_col_blocks)
    def _fetch_next():
        for c in copies(j + 1):
            c.start()


def _ws_body(*refs, dot_lhs, w_geoms, n_lhs, n_extra, n_out, tms, epilogue):
    n_dots = len(dot_lhs)
    pos = 0
    def take(n):
        nonlocal pos
        out = refs[pos:pos + n]
        pos += n
        return out
    lhs_p, lhs_s = take(n_lhs), take(n_lhs)
    w_refs = take(n_dots)
    ex_p, ex_s = take(n_extra), take(n_extra)
    out_p, out_s = take(n_out), take(n_out)
    fetch_refs = take(n_dots)
    wbf_refs = take(n_dots)
    stage_bufs = (take(n_dots), take(n_dots))
    (sems,) = take(1)
    j = pl.program_id(0)
    i = pl.program_id(1)

    def run_rows(lhs_refs, ex_refs, out_refs, tm):
        chunks = _row_chunks(tm)
        n = chunks[0].stop
        for s in range(len(chunks) + 1):
            if s < len(chunks):
                for li, s_ref, r_ref in zip(dot_lhs, wbf_refs, stage_bufs[s % 2]):
                    r_ref[0:n, :] = jnp.dot(lhs_refs[li][chunks[s], :], s_ref[...], preferred_element_type=F32)
            if s >= 1:
                rows = chunks[s - 1]
                accs = [r_ref[0:n, :] for r_ref in stage_bufs[(s - 1) % 2]]
                for o_ref, val in zip(out_refs, epilogue(accs, ex_refs, rows, j)):
                    o_ref[rows, :] = val.astype(o_ref.dtype)

    @pl.when(i == 0)
    def _new_column_block():
        _advance_weights(j, pl.num_programs(0), w_refs, w_geoms, fetch_refs, wbf_refs, sems)
        run_rows(lhs_s, ex_s, out_s, tms[1])

    run_rows(lhs_p, ex_p, out_p, tms[0])


def _ws_matmul(lhs, dots, extras, epilogue, out_dtypes, *, n_cols, tm, tn, name):
    m_p = lhs[0][0][0].shape[0]
    m_s = lhs[0][0][1].shape[0]
    in_specs, args = [], []
    for grp in (0, 1):
        for arrs, kb, ks in lhs:
            if grp == 0:
                in_specs.append(pl.BlockSpec((tm, ks), lambda j, i, kb=kb: (i, kb)))
            else:
                in_specs.append(pl.BlockSpec((m_s, ks), lambda j, i, kb=kb: (0, kb)))
            args.append(arrs[grp])
    w_geoms = []
    for li, w, off in dots:
        _, kb, ks = lhs[li]
        in_specs.append(pl.BlockSpec(memory_space=pl.ANY))
        args.append(w)
        w_geoms.append((kb * ks, off // tn))
    for grp in (0, 1):
        for arrs, cols, imap in extras:
            if grp == 0:
                in_specs.append(pl.BlockSpec((tm, cols), imap))
            else:
                in_specs.append(pl.BlockSpec((m_s, cols), (lambda j, i: (0, j)) if cols == tn else (lambda j, i: (0, 0))))
            args.append(arrs[grp])
    out_specs = ([pl.BlockSpec((tm, tn), lambda j, i: (i, j)) for _ in out_dtypes]
                 + [pl.BlockSpec((m_s, tn), lambda j, i: (0, j)) for _ in out_dtypes])
    out_shape = ([jax.ShapeDtypeStruct((m_p, n_cols), dt) for dt in out_dtypes]
                 + [jax.ShapeDtypeStruct((m_s, n_cols), dt) for dt in out_dtypes])
    body = functools.partial(_ws_body, dot_lhs=tuple(l for l, _, _ in dots), w_geoms=tuple(w_geoms),
                             n_lhs=len(lhs), n_extra=len(extras), n_out=len(out_dtypes), tms=(tm, m_s),
                             epilogue=epilogue)
    outs = pl.pallas_call(
        body,
        grid=(n_cols // tn, m_p // tm),
        in_specs=in_specs,
        out_specs=out_specs,
        out_shape=out_shape,
        scratch_shapes=([pltpu.VMEM((lhs[li][2], tn), F32) for li, _, _ in dots]
                        + [pltpu.VMEM((lhs[li][2], tn), BF16) for li, _, _ in dots]
                        + [pltpu.VMEM((min(tm, ROW_CHUNK), tn), F32) for _ in range(2 * len(dots))]
                        + [pltpu.SemaphoreType.DMA((len(dots),))]),
        compiler_params=_params(2),
        name=name,
    )(*args)
    n = len(out_dtypes)
    return list(zip(outs[:n], outs[n:]))


def _glu_epilogue(accs, ex_refs, rows, j):
    return [accs[0] * _sigmoid(accs[1])]


def _rope_epilogue(accs, ex_refs, rows, j, *, tn):
    cos_ref, sin_ref = ex_refs
    z = accs[0]
    rotated = j < (Q_WIDTH + KV_WIDTH) // tn
    cos_t = jnp.where(rotated, cos_ref[rows, :], 1.0)
    sin_t = jnp.where(rotated, sin_ref[rows, :], 0.0)
    lane = lax.broadcasted_iota(jnp.int32, cos_t.shape, 1)
    half = ROT_DIM // 2
    outs = []
    for c in range(tn // HEAD_DIM):
        x = z[:, c * HEAD_DIM:(c + 1) * HEAD_DIM]
        partner = jnp.where(lane < half, pltpu.roll(x, HEAD_DIM - half, 1), pltpu.roll(x, half, 1))
        outs.append(x * cos_t + partner * sin_t)
    return [jnp.concatenate(outs, axis=1)]


def _merge_epilogue(accs, ex_refs, rows, j):
    gate_a, br_a, gate_b, br_b = accs
    return [_sigmoid(gate_a) * br_a + _sigmoid(gate_b) * br_b]


def _residual_epilogue(accs, ex_refs, rows, j):
    return [ex_refs[0][rows, :] + accs[0]]


def _ln_swish(y, g_ref, b_ref):
    mu = jnp.mean(y, axis=-1, keepdims=True)
    yc = y - mu
    var = jnp.mean(yc * yc, axis=-1, keepdims=True)
    t = yc * lax.rsqrt(var + EPS) * g_ref[...] + b_ref[...]
    return t * _sigmoid(t)


CONV_ROWS = 32


def _conv_row_chunk(rc, xx_ref, sh_ref, w_ref, b_ref, y_ref):
    lanes = 256
    base = CONV_HALO - (CONV_K - 1)
    r0 = rc * CONV_ROWS
    for cc in range(CONV_WIDTH // lanes):
        cs = slice(cc * lanes, (cc + 1) * lanes)
        acc = jnp.zeros((CONV_ROWS, lanes), F32) + b_ref[:, cs]
        for k in range(CONV_K):
            a, b = divmod(base + k, SUBLANES)
            start = r0 + a * SUBLANES
            if b == 0:
                x = xx_ref[start:start + CONV_ROWS, cs]
            else:
                x = sh_ref[b - 1, start:start + CONV_ROWS, cs]
            acc = acc + w_ref[k:k + 1, cs] * x
        y_ref[r0:r0 + CONV_ROWS, cs] = acc


def _attn_unit(q_ref, k_parts, v_parts, sink_ref, o_ref, q_rows, kh, mask):
    ks = slice(kh * HEAD_DIM, (kh + 1) * HEAD_DIM)
    kk = jnp.concatenate([part[:, ks] for part in k_parts], axis=0).astype(BF16)
    vv = jnp.concatenate([part[:, ks] for part in v_parts], axis=0).astype(BF16)
    heads = [kh * GROUP + g for g in range(GROUP)]
    qg = jnp.concatenate([q_ref[q_rows, h * HEAD_DIM:(h + 1) * HEAD_DIM] for h in heads], axis=0).astype(BF16)
    sink = jnp.concatenate([jnp.broadcast_to(sink_ref[h:h + 1, 0:1], (BLOCK, 1)) for h in heads], axis=0)
    s = lax.dot_general(qg, kk, (((1,), (1,)), ((), ())), preferred_element_type=F32) * SCALE
    p = _softmax_with_sink(jnp.where(mask, s, -jnp.inf), sink)
    o = jnp.dot(p.astype(BF16), vv, preferred_element_type=F32)
    for g, h in enumerate(heads):
        o_ref[q_rows, h * HEAD_DIM:(h + 1) * HEAD_DIM] = o[g * BLOCK:(g + 1) * BLOCK].astype(o_ref.dtype)


def _branches_prompt_body(cur_ref, prev_ref, w_ref, b_ref, g_ref, beta_ref,
                          q_ref, kc_ref, kp_ref, vc_ref, vp_ref, sink_ref,
                          c_ref, o_ref, xx_ref, sh_ref, y_ref, *, tl):
    first = pl.program_id(1) == 0
    xx_ref[0:CONV_HALO, :] = jnp.where(first, 0.0, prev_ref[...])
    xx_ref[CONV_HALO:, :] = cur_ref[...]
    n_sh = sh_ref.shape[1]
    for b in range(1, SUBLANES):
        sh_ref[b - 1] = xx_ref[b:b + n_sh, :]

    rows = GROUP * BLOCK
    r = lax.broadcasted_iota(jnp.int32, (rows, 2 * BLOCK), 0) % BLOCK
    c = lax.broadcasted_iota(jnp.int32, (rows, 2 * BLOCK), 1)
    diff = (r + BLOCK) - c
    band = (diff >= 0) & (diff < WINDOW)
    band_first = band & ((c >= BLOCK) | jnp.logical_not(first))

    units = []
    for blk in range(tl // BLOCK):
        q_rows = slice(blk * BLOCK, (blk + 1) * BLOCK)
        if blk == 0:
            k_parts, v_parts, mask = (kp_ref, kc_ref.at[q_rows]), (vp_ref, vc_ref.at[q_rows]), band_first
        else:
            kv_rows = slice((blk - 1) * BLOCK, (blk + 1) * BLOCK)
            k_parts, v_parts, mask = (kc_ref.at[kv_rows],), (vc_ref.at[kv_rows],), band
        units += [(q_rows, kh, k_parts, v_parts, mask) for kh in range(N_KV_HEADS)]

    n_chunks = tl // CONV_ROWS
    assert len(units) == n_chunks
    always = pl.program_id(1) < pl.num_programs(1)
    for rc, (q_rows, kh, k_parts, v_parts, mask) in enumerate(units):
        @pl.when(always)
        def _pair(rc=rc, q_rows=q_rows, kh=kh, k_parts=k_parts, v_parts=v_parts, mask=mask):
            _conv_row_chunk(rc, xx_ref, sh_ref, w_ref, b_ref, y_ref)
            _attn_unit(q_ref, k_parts, v_parts, sink_ref, o_ref, q_rows, kh, mask)
    c_ref[...] = _ln_swish(y_ref[...], g_ref, beta_ref).astype(c_ref.dtype)


def _branches_prompt(u, qkv, conv_w, conv_b, ln_g, ln_b, sinks, *, n_batch, seq, tl):
    c = CONV_WIDTH
    nl = seq // tl
    halo_per_tile = tl // CONV_HALO
    blocks_per_tile = tl // BLOCK
    k_col = Q_WIDTH // KV_WIDTH
    row = lambda a: a.reshape(1, c)
    vec = pl.BlockSpec((1, c), lambda b, l: (0, 0))
    tile = lambda width, col: pl.BlockSpec((tl, width), lambda b, l: (b * nl + l, col))
    prev_block = lambda col: pl.BlockSpec(
        (BLOCK, KV_WIDTH), lambda b, l: (jnp.maximum((b * nl + l) * blocks_per_tile - 1, 0), col))
    return pl.pallas_call(
        functools.partial(_branches_prompt_body, tl=tl),
        grid=(n_batch, nl),
        in_specs=[tile(c, 0),
                  pl.BlockSpec((CONV_HALO, c),
                               lambda b, l: (jnp.maximum((b * nl + l) * halo_per_tile - 1, 0), 0)),
                  pl.BlockSpec((CONV_K, c), lambda b, l: (0, 0)),
                  vec, vec, vec,
                  tile(Q_WIDTH, 0),
                  tile(KV_WIDTH, k_col), prev_block(k_col), tile(KV_WIDTH, k_col + 1), prev_block(k_col + 1),
                  pl.BlockSpec((N_HEADS, HEAD_DIM), lambda b, l: (0, 0))],
        out_specs=[tile(c, 0), tile(Q_WIDTH, 0)],
        out_shape=[jax.ShapeDtypeStruct((n_batch * seq, c), BF16),
                   jax.ShapeDtypeStruct((n_batch * seq, Q_WIDTH), BF16)],
        scratch_shapes=[pltpu.VMEM((tl + CONV_HALO, c), F32),
                        pltpu.VMEM((SUBLANES - 1, tl + CONV_HALO - SUBLANES, c), F32),
                        pltpu.VMEM((tl, c), F32)],
        compiler_params=_params(2),
        name="branches_prompt",
    )(u, u, conv_w, row(conv_b), row(ln_g), row(ln_b), qkv, qkv, qkv, qkv, qkv, sinks)


def _conv_a_sample_body(hist_ref, u_ref, w_ref, b_ref, g_ref, beta_ref, o_ref, new_ref):
    u = u_ref[...]
    y = u * w_ref[CONV_K - 1:CONV_K, :] + b_ref[...]
    for k in range(CONV_K - 1):
        y = y + hist_ref[k] * w_ref[k:k + 1, :]
    o_ref[...] = _ln_swish(y, g_ref, beta_ref).astype(o_ref.dtype)
    new_ref[0:CONV_K - 2] = hist_ref[1:CONV_K - 1]
    new_ref[CONV_K - 2] = u


def _conv_a_sample(hist_t, u, conv_w, conv_b, ln_g, ln_b, *, bs):
    n, c = u.shape
    row = lambda a: a.reshape(1, c)
    vec = pl.BlockSpec((1, c), lambda b: (0, 0))
    hist_spec = pl.BlockSpec((CONV_K - 1, bs, c), lambda b: (0, b, 0))
    return pl.pallas_call(
        _conv_a_sample_body,
        grid=(n // bs,),
        in_specs=[hist_spec,
                  pl.BlockSpec((bs, c), lambda b: (b, 0)),
                  pl.BlockSpec((CONV_K, c), lambda b: (0, 0)),
                  vec, vec, vec],
        out_specs=[pl.BlockSpec((bs, c), lambda b: (b, 0)), hist_spec],
        out_shape=[jax.ShapeDtypeStruct((n, c), BF16), jax.ShapeDtypeStruct(hist_t.shape, F32)],
        compiler_params=_params(1),
        name="conv_a_sample",
    )(hist_t, u, conv_w, row(conv_b), row(ln_g), row(ln_b))


def _softmax_with_sink(s, sink):
    m = jnp.maximum(jnp.max(s, axis=-1, keepdims=True), sink)
    p = jnp.exp(s - m)
    return p / (jnp.sum(p, axis=-1, keepdims=True) + jnp.exp(sink - m))


def _attn_sample_body(qkv_ref, ck_ref, cv_ref, sink_ref, o_ref, ko_ref, vo_ref, *, bs):
    n_rows = WINDOW * N_KV_HEADS
    keep = n_rows - N_KV_HEADS
    k_row = N_HEADS
    v_row = N_HEADS + N_KV_HEADS
    head = lax.broadcasted_iota(jnp.int32, (N_HEADS, n_rows), 0)
    col = lax.broadcasted_iota(jnp.int32, (N_HEADS, n_rows), 1)
    same_group = (col % N_KV_HEADS) == (head // GROUP)
    sink = sink_ref[:, 0:1]

    def per_sample(b, carry):
        ko_ref[b, 0:keep, :] = ck_ref[b, N_KV_HEADS:n_rows, :]
        ko_ref[b, keep:n_rows, :] = qkv_ref[b, k_row:k_row + N_KV_HEADS, :]
        vo_ref[b, 0:keep, :] = cv_ref[b, N_KV_HEADS:n_rows, :]
        vo_ref[b, keep:n_rows, :] = qkv_ref[b, v_row:v_row + N_KV_HEADS, :]
        q = qkv_ref[b, 0:N_HEADS, :].astype(BF16)
        kk = ko_ref[b].astype(BF16)
        vv = vo_ref[b].astype(BF16)
        s = lax.dot_general(q, kk, (((1,), (1,)), ((), ())), preferred_element_type=F32) * SCALE
        p = _softmax_with_sink(jnp.where(same_group, s, -jnp.inf), sink)
        o_ref[b] = jnp.dot(p.astype(BF16), vv, preferred_element_type=F32).astype(o_ref.dtype)
        return carry

    lax.fori_loop(0, bs, per_sample, 0)


def _attn_sample(qkv, cache_k, cache_v, sinks, *, bs):
    n = qkv.shape[0]
    n_rows = QKV_WIDTH // HEAD_DIM
    cache = pl.BlockSpec((bs, WINDOW * N_KV_HEADS, HEAD_DIM), lambda b: (b, 0, 0))
    o, k_new, v_new = pl.pallas_call(
        functools.partial(_attn_sample_body, bs=bs),
        grid=(n // bs,),
        in_specs=[pl.BlockSpec((bs, n_rows, HEAD_DIM), lambda b: (b, 0, 0)), cache, cache,
                  pl.BlockSpec((N_HEADS, HEAD_DIM), lambda b: (0, 0))],
        out_specs=[pl.BlockSpec((bs, N_HEADS, HEAD_DIM), lambda b: (b, 0, 0)), cache, cache],
        out_shape=[jax.ShapeDtypeStruct((n, N_HEADS, HEAD_DIM), BF16),
                   jax.ShapeDtypeStruct(cache_k.shape, F32), jax.ShapeDtypeStruct(cache_v.shape, F32)],
        compiler_params=_params(1),
        name="attn_sample",
    )(qkv.reshape(n, n_rows, HEAD_DIM), cache_k, cache_v, sinks)
    return o.reshape(n, Q_WIDTH), k_new, v_new


def _ffn_body(xp_ref, xs_ref, wg_ref, wv_ref, cwg_ref, cwv_ref, cbg_ref, cbv_ref,
              hg_ref, hv_ref,
              fp_ref, tg_ref, tv_ref, fs_ref, ng_ref, nv_ref,
              wg_fetch, wv_fetch, wg_bf, wv_bf, eg0_ref, eg1_ref, ev0_ref, ev1_ref, sems,
              *, tm, tiles_per_seq, up_col_block0):
    i = pl.program_id(1)
    chunks = _row_chunks(tm)
    rc = chunks[0].stop
    assert len(chunks) % 2 == 0
    stage_bufs = ((eg0_ref, ev0_ref), (eg1_ref, ev1_ref))
    last_bufs = stage_bufs[(len(chunks) - 1) % 2]
    head = slice(0, SUBLANES)
    tail = slice(rc, rc + SUBLANES)

    @pl.when(i == 0)
    def _new_column_block():
        _advance_weights(pl.program_id(0), pl.num_programs(0), (wg_ref, wv_ref), ((0, 0), (0, up_col_block0)),
                         (wg_fetch, wv_fetch), (wg_bf, wv_bf), sems)
        x = xs_ref[...]
        ys = []
        for w_bf, cw_ref, cb_ref, h_ref, n_ref in ((wg_bf, cwg_ref, cbg_ref, hg_ref, ng_ref),
                                                   (wv_bf, cwv_ref, cbv_ref, hv_ref, nv_ref)):
            d = jnp.dot(x, w_bf[...], preferred_element_type=F32)
            h0 = h_ref[:, 0, :]
            h1 = h_ref[:, 1, :]
            n_ref[:, 0, :] = h1
            n_ref[:, 1, :] = d
            ys.append(cw_ref[0:1, :] * h0 + cw_ref[1:2, :] * h1 + cw_ref[2:3, :] * d + cb_ref[...])
        fs_ref[...] = (ys[0] * _sigmoid(ys[0]) * ys[1]).astype(fs_ref.dtype)

    @pl.when(i % tiles_per_seq == 0)
    def _zero_history():
        for e_ref in last_bufs:
            e_ref[tail, :] = jnp.zeros((SUBLANES, e_ref.shape[1]), F32)

    def conv(e_ref, cw_ref, cb_ref):
        y = cb_ref[...]
        for k in range(FFN_CONV_K):
            r0 = SUBLANES - (FFN_CONV_K - 1) + k
            y = y + cw_ref[k:k + 1, :] * e_ref[r0:r0 + rc, :]
        return y

    for s in range(len(chunks) + 1):
        if s < len(chunks):
            x = xp_ref[chunks[s], :]
            for e_ref, e_prev, w_bf in zip(stage_bufs[s % 2], stage_bufs[(s - 1) % 2], (wg_bf, wv_bf)):
                e_ref[head, :] = e_prev[tail, :]
                e_ref[SUBLANES:, :] = jnp.dot(x, w_bf[...], preferred_element_type=F32)
        if s >= 1:
            eg_ref, ev_ref = stage_bufs[(s - 1) % 2]
            yg = conv(eg_ref, cwg_ref, cbg_ref)
            yv = conv(ev_ref, cwv_ref, cbv_ref)
            fp_ref[chunks[s - 1], :] = (yg * _sigmoid(yg) * yv).astype(fp_ref.dtype)
    for e_ref, t_ref in zip(last_bufs, (tg_ref, tv_ref)):
        t_ref[0] = e_ref[tail, :]


def _ffn_up(x_p, x_s, w_up, conv_w, conv_b, hist, *, tm, tn, seq):
    m, d = x_p.shape
    m_s = x_s.shape[0]
    half_blocks = D_FF // tn
    tiles_per_seq = seq // tm
    col = lambda off: (lambda j, i: (0, off + j))
    conv_b = conv_b.reshape(1, 2 * D_FF)
    in_specs = [pl.BlockSpec((tm, d), lambda j, i: (i, 0)),
                pl.BlockSpec((m_s, d), lambda j, i: (0, 0)),
                pl.BlockSpec(memory_space=pl.ANY), pl.BlockSpec(memory_space=pl.ANY),
                pl.BlockSpec((FFN_CONV_K, tn), col(0)), pl.BlockSpec((FFN_CONV_K, tn), col(half_blocks)),
                pl.BlockSpec((1, tn), col(0)), pl.BlockSpec((1, tn), col(half_blocks))]
    n_hist = FFN_CONV_K - 1
    in_specs += [pl.BlockSpec((m_s, n_hist, tn), lambda j, i: (0, 0, j)),
                 pl.BlockSpec((m_s, n_hist, tn), lambda j, i: (0, 0, half_blocks + j))]
    args = [x_p, x_s, w_up, w_up, conv_w, conv_w, conv_b, conv_b, hist, hist]
    tail_spec = pl.BlockSpec((1, SUBLANES, tn), lambda j, i: (i // tiles_per_seq, 0, j))
    tail_shape = jax.ShapeDtypeStruct((m // seq, SUBLANES, D_FF), F32)
    s_spec = pl.BlockSpec((m_s, tn), lambda j, i: (0, j))
    hist_spec = pl.BlockSpec((m_s, n_hist, tn), lambda j, i: (0, 0, j))
    hist_shape = jax.ShapeDtypeStruct((m_s, n_hist, D_FF), F32)
    return pl.pallas_call(
        functools.partial(_ffn_body, tm=tm, tiles_per_seq=tiles_per_seq, up_col_block0=half_blocks),
        grid=(half_blocks, m // tm),
        in_specs=in_specs,
        out_specs=[pl.BlockSpec((tm, tn), lambda j, i: (i, j)), tail_spec, tail_spec, s_spec, hist_spec, hist_spec],
        out_shape=[jax.ShapeDtypeStruct((m, D_FF), BF16), tail_shape, tail_shape,
                   jax.ShapeDtypeStruct((m_s, D_FF), BF16), hist_shape, hist_shape],
        scratch_shapes=([pltpu.VMEM((d, tn), F32)] * 2 + [pltpu.VMEM((d, tn), BF16)] * 2
                        + [pltpu.VMEM((min(tm, ROW_CHUNK) + SUBLANES, tn), F32)] * 4
                        + [pltpu.SemaphoreType.DMA((2,))]),
        compiler_params=_params(2),
        name="ffn_up",
    )(*args)


def kernel(x_prompt, x_sample, cache_k_win, cache_v_win, state_conv_a, state_conv_ffn, norm1_g, w_in,
           conv_a_w, conv_a_b, ln_a_g, ln_a_b, w_a_out, attn_sinks, w_attn_out, w_o, norm2_g, w_up,
           conv_f_w, conv_f_b, w_down, norm_f_g):
    n_batch, seq, d = x_prompt.shape
    n_dec = x_sample.shape[0]
    assert x_sample.shape[1] == 1 and w_in.shape[0] == 1
    w_in, w_a_out, w_attn_out, w_o, w_down = w_in[0], w_a_out[0], w_attn_out[0], w_o[0], w_down[0]
    sinks = jnp.broadcast_to(attn_sinks[0][:, None], (N_HEADS, HEAD_DIM))
    conv_a_args = (conv_a_w[0], conv_a_b[0], ln_a_g[0], ln_a_b[0])
    tm, tr = 1024, 256
    tile = lambda j, i: (i, j)
    full = lambda arrs: (arrs, 0, arrs[0].shape[1])

    x = (x_prompt.reshape(n_batch * seq, d), x_sample.reshape(n_dec, d))
    xn = tuple(_rmsnorm(a, norm1_g[0], BF16, tr) for a in x)
    (u,) = _ws_matmul([full(xn)], [(0, w_in, 0), (0, w_in, CONV_WIDTH)], [], _glu_epilogue, [F32],
                      n_cols=CONV_WIDTH, tm=tm, tn=512, name="in_glu")
    cos_p, sin_p = _rope_tables(seq, 0, 1)
    cos_s, sin_s = _rope_tables(n_dec, PAST_LEN, 0)
    rope_map = lambda j, i: (i % (seq // tm), 0)
    tn_qkv = KV_WIDTH
    (qkv,) = _ws_matmul([full(xn)], [(0, w_in, A_END)],
                        [((cos_p, cos_s), HEAD_DIM, rope_map), ((sin_p, sin_s), HEAD_DIM, rope_map)],
                        functools.partial(_rope_epilogue, tn=tn_qkv), [F32],
                        n_cols=QKV_WIDTH, tm=tm, tn=tn_qkv, name="in_qkv_rope")

    c_p, o_p = _branches_prompt(u[0], qkv[0], *conv_a_args, sinks, n_batch=n_batch, seq=seq, tl=256)
    hist_a = jnp.transpose(state_conv_a[0], (1, 0, 2))
    c_s, hist_a_new = _conv_a_sample(hist_a, u[1], *conv_a_args, bs=8)
    cache_shape = (n_dec, WINDOW * N_KV_HEADS, HEAD_DIM)
    o_s, k_win_s, v_win_s = _attn_sample(qkv[1], cache_k_win[0].reshape(cache_shape),
                                         cache_v_win[0].reshape(cache_shape), sinks, bs=16)

    (merged,) = _ws_matmul([full(xn), full((c_p, c_s)), full((o_p, o_s))],
                           [(0, w_in, V_END), (1, w_a_out, 0), (0, w_in, V_END + D_MODEL), (2, w_attn_out, 0)],
                           [], _merge_epilogue, [BF16],
                           n_cols=D_MODEL, tm=256, tn=512, name="gated_merge")
    (h,) = _ws_matmul([full(merged)], [(0, w_o, 0)], [(x, 512, tile)], _residual_epilogue, [F32],
                      n_cols=D_MODEL, tm=tm, tn=512, name="out_proj")

    hn = tuple(_rmsnorm(a, norm2_g[0], BF16, tr) for a in h)
    f_p, tail_g, tail_v, f_s, hist_g, hist_v = _ffn_up(hn[0], hn[1], w_up[0], conv_f_w[0], conv_f_b[0],
                                                       state_conv_ffn[0],
                                                     tm=tm, tn=256, seq=seq)
    y = h
    k_half = D_FF // 2
    for kb in range(2):
        (y,) = _ws_matmul([((f_p, f_s), kb, k_half)], [(0, w_down, 0)], [(y, 512, tile)], _residual_epilogue,
                          [F32], n_cols=D_MODEL, tm=tm, tn=512, name="ffn_down")
    y_p, y_s = (_rmsnorm(a, norm_f_g, F32, tr) for a in y)

    qkv_p = qkv[0].reshape(n_batch, seq, QKV_WIDTH)
    win_shape = (1, n_batch, WINDOW, N_KV_HEADS, HEAD_DIM)
    k_win_p = qkv_p[:, seq - WINDOW:, Q_WIDTH:Q_WIDTH + KV_WIDTH].reshape(win_shape)
    v_win_p = qkv_p[:, seq - WINDOW:, Q_WIDTH + KV_WIDTH:].reshape(win_shape)
    conv_a_p = u[0].reshape(n_batch, seq, CONV_WIDTH)[None, :, seq - (CONV_K - 1):]
    keep = SUBLANES - (FFN_CONV_K - 1)
    conv_ffn_p = jnp.concatenate([tail_g[:, keep:], tail_v[:, keep:]], axis=-1)[None]
    conv_a_s = jnp.transpose(hist_a_new, (1, 0, 2))[None]
    conv_ffn_s = jnp.concatenate([hist_g, hist_v], axis=-1)[None]
    return (y_p.reshape(n_batch, seq, d), y_s.reshape(n_dec, 1, d), k_win_p, v_win_p, conv_a_p, conv_ffn_p,
            k_win_s.reshape(cache_k_win.shape), v_win_s.reshape(cache_v_win.shape), conv_a_s, conv_ffn_s)
```

```python
import functools

import jax
import jax.numpy as jnp
from jax import lax
from jax.experimental import pallas as pl
from jax.experimental.pallas import tpu as pltpu

F32 = jnp.float32
BF16 = jnp.bfloat16

D_MODEL = 4096
CONV_WIDTH = D_MODEL // 2
CONV_K = 31
HEAD_DIM = 128
N_HEADS = (D_MODEL // 2) // HEAD_DIM
N_KV_HEADS = N_HEADS // 4
GROUP = N_HEADS // N_KV_HEADS
ROT_DIM = HEAD_DIM // 4
ROPE_THETA = 500000.0
WINDOW = 128
BLOCK = 128
D_FF = 11008
FFN_CONV_K = 3
EPS = 1e-6
PAST_LEN = 8192
Q_WIDTH = N_HEADS * HEAD_DIM
KV_WIDTH = N_KV_HEADS * HEAD_DIM
A_END = 2 * CONV_WIDTH
V_END = A_END + Q_WIDTH + 2 * KV_WIDTH
QKV_WIDTH = Q_WIDTH + 2 * KV_WIDTH
SCALE = HEAD_DIM ** -0.5

V7X_VMEM_BYTES = 64 * 1024 * 1024
VMEM_LIMIT = V7X_VMEM_BYTES - 8 * 1024 * 1024
SUBLANES = 8
LANES = 128
CONV_HALO = 32
ROW_CHUNK = 128


def _params(n_axes):
    return pltpu.CompilerParams(dimension_semantics=("arbitrary",) * n_axes,
                                vmem_limit_bytes=VMEM_LIMIT)


def _sigmoid(x):
    return 1.0 / (1.0 + jnp.exp(-x))


def _row_chunks(tm):
    rc = min(tm, ROW_CHUNK)
    return [slice(r * rc, (r + 1) * rc) for r in range(tm // rc)]


def _rmsnorm_body(x_ref, g_ref, o_ref):
    x = x_ref[...]
    ms = jnp.mean(x * x, axis=-1, keepdims=True)
    o_ref[...] = (x * lax.rsqrt(ms + EPS) * g_ref[...]).astype(o_ref.dtype)


def _rmsnorm(x, g, out_dtype, tr):
    m, d = x.shape
    tr = min(tr, m)
    return pl.pallas_call(
        _rmsnorm_body,
        grid=(m // tr,),
        in_specs=[pl.BlockSpec((tr, d), lambda i: (i, 0)),
                  pl.BlockSpec((1, d), lambda i: (0, 0))],
        out_specs=pl.BlockSpec((tr, d), lambda i: (i, 0)),
        out_shape=jax.ShapeDtypeStruct((m, d), out_dtype),
        compiler_params=_params(1),
        name="rmsnorm",
    )(x, g.reshape(1, d))


def _rope_table_body(inv_ref, sign_ref, cos_ref, sin_ref, *, pos0, pos_step):
    rows = cos_ref.shape[0]
    i = pl.program_id(0)
    r = lax.broadcasted_iota(jnp.int32, (rows, HEAD_DIM), 0) + i * rows
    lane = lax.broadcasted_iota(jnp.int32, (rows, HEAD_DIM), 1)
    pos = (pos0 + pos_step * r).astype(F32)
    ang = pos * inv_ref[...]
    cos_ref[...] = jnp.where(lane < ROT_DIM, jnp.cos(ang), 1.0)
    sin_ref[...] = sign_ref[...] * jnp.sin(ang)


def _rope_tables(n_rows, pos0, pos_step):
    half = ROT_DIM // 2
    inv = ROPE_THETA ** (-2.0 * jnp.arange(half, dtype=F32) / ROT_DIM)
    zeros = jnp.zeros((HEAD_DIM - ROT_DIM,), F32)
    inv_full = jnp.concatenate([inv, inv, zeros]).reshape(1, HEAD_DIM)
    sign = jnp.concatenate([-jnp.ones((half,), F32), jnp.ones((half,), F32), zeros]).reshape(1, HEAD_DIM)
    tr = min(n_rows, 512)
    spec_c = pl.BlockSpec((1, HEAD_DIM), lambda i: (0, 0))
    spec_t = pl.BlockSpec((tr, HEAD_DIM), lambda i: (i, 0))
    return pl.pallas_call(
        functools.partial(_rope_table_body, pos0=pos0, pos_step=pos_step),
        grid=(n_rows // tr,),
        in_specs=[spec_c, spec_c],
        out_specs=[spec_t, spec_t],
        out_shape=[jax.ShapeDtypeStruct((n_rows, HEAD_DIM), F32)] * 2,
        compiler_params=_params(1),
        name="rope_tables",
    )(inv_full, sign)


def _weight_copy(w_hbm, fetch_ref, sem, row0, col0):
    ks, tn = fetch_ref.shape
    return pltpu.make_async_copy(w_hbm.at[pl.ds(row0, ks), pl.ds(col0, tn)], fetch_ref, sem)


def _advance_weights(j, n_col_blocks, w_refs, geoms, fetch_refs, wbf_refs, sems):
    def copies(col_block):
        out = []
        for d, (w_hbm, fetch_ref, (row0, col_block0)) in enumerate(zip(w_refs, fetch_refs, geoms)):
            col0 = pl.multiple_of((col_block0 + col_block) * fetch_ref.shape[1], LANES)
            out.append(_weight_copy(w_hbm, fetch_ref, sems.at[d], row0, col0))
        return out

    @pl.when(j == 0)
    def _fetch_first():
        for c in copies(0):
            c.start()

    for c in copies(j):
        c.wait()
    for fetch_ref, s_ref in zip(fetch_refs, wbf_refs):
        s_ref[...] = fetch_ref[...].astype(BF16)

    @pl.when(j + 1 < n_col_blocks)
    def _fetch_next():
        for c in copies(j + 1):
            c.start()


def _ws_body(*refs, dot_lhs, w_geoms, n_lhs, n_extra, n_out, tms, epilogue):
    n_dots = len(dot_lhs)
    pos = 0
    def take(n):
        nonlocal pos
        out = refs[pos:pos + n]
        pos += n
        return out
    lhs_p, lhs_s = take(n_lhs), take(n_lhs)
    w_refs = take(n_dots)
    ex_p, ex_s = take(n_extra), take(n_extra)
    out_p, out_s = take(n_out), take(n_out)
    fetch_refs = take(n_dots)
    wbf_refs = take(n_dots)
    stage_bufs = (take(n_dots), take(n_dots))
    (sems,) = take(1)
    j = pl.program_id(0)
    i = pl.program_id(1)

    def run_rows(lhs_refs, ex_refs, out_refs, tm):
        chunks = _row_chunks(tm)
        n = chunks[0].stop
        for s in range(len(chunks) + 1):
            if s < len(chunks):
                for li, s_ref, r_ref in zip(dot_lhs, wbf_refs, stage_bufs[s % 2]):
                    r_ref[0:n, :] = jnp.dot(lhs_refs[li][chunks[s], :], s_ref[...], preferred_element_type=F32)
            if s >= 1:
                rows = chunks[s - 1]
                accs = [r_ref[0:n, :] for r_ref in stage_bufs[(s - 1) % 2]]
                for o_ref, val in zip(out_refs, epilogue(accs, ex_refs, rows, j)):
                    o_ref[rows, :] = val.astype(o_ref.dtype)

    @pl.when(i == 0)
    def _new_column_block():
        _advance_weights(j, pl.num_programs(0), w_refs, w_geoms, fetch_refs, wbf_refs, sems)
        run_rows(lhs_s, ex_s, out_s, tms[1])

    run_rows(lhs_p, ex_p, out_p, tms[0])


def _ws_matmul(lhs, dots, extras, epilogue, out_dtypes, *, n_cols, tm, tn, name):
    m_p = lhs[0][0][0].shape[0]
    m_s = lhs[0][0][1].shape[0]
    in_specs, args = [], []
    for grp in (0, 1):
        for arrs, kb, ks in lhs:
            if grp == 0:
                in_specs.append(pl.BlockSpec((tm, ks), lambda j, i, kb=kb: (i, kb)))
            else:
                in_specs.append(pl.BlockSpec((m_s, ks), lambda j, i, kb=kb: (0, kb)))
            args.append(arrs[grp])
    w_geoms = []
    for li, w, off in dots:
        _, kb, ks = lhs[li]
        in_specs.append(pl.BlockSpec(memory_space=pl.ANY))
        args.append(w)
        w_geoms.append((kb * ks, off // tn))
    for grp in (0, 1):
        for arrs, cols, imap in extras:
            if grp == 0:
                in_specs.append(pl.BlockSpec((tm, cols), imap))
            else:
                in_specs.append(pl.BlockSpec((m_s, cols), (lambda j, i: (0, j)) if cols == tn else (lambda j, i: (0, 0))))
            args.append(arrs[grp])
    out_specs = ([pl.BlockSpec((tm, tn), lambda j, i: (i, j)) for _ in out_dtypes]
                 + [pl.BlockSpec((m_s, tn), lambda j, i: (0, j)) for _ in out_dtypes])
    out_shape = ([jax.ShapeDtypeStruct((m_p, n_cols), dt) for dt in out_dtypes]
                 + [jax.ShapeDtypeStruct((m_s, n_cols), dt) for dt in out_dtypes])
    body = functools.partial(_ws_body, dot_lhs=tuple(l for l, _, _ in dots), w_geoms=tuple(w_geoms),
                             n_lhs=len(lhs), n_extra=len(extras), n_out=len(out_dtypes), tms=(tm, m_s),
                             epilogue=epilogue)
    outs = pl.pallas_call(
        body,
        grid=(n_cols // tn, m_p // tm),
        in_specs=in_specs,
        out_specs=out_specs,
        out_shape=out_shape,
        scratch_shapes=([pltpu.VMEM((lhs[li][2], tn), F32) for li, _, _ in dots]
                        + [pltpu.VMEM((lhs[li][2], tn), BF16) for li, _, _ in dots]
                        + [pltpu.VMEM((min(tm, ROW_CHUNK), tn), F32) for _ in range(2 * len(dots))]
                        + [pltpu.SemaphoreType.DMA((len(dots),))]),
        compiler_params=_params(2),
        name=name,
    )(*args)
    n = len(out_dtypes)
    return list(zip(outs[:n], outs[n:]))


def _glu_epilogue(accs, ex_refs, rows, j):
    return [accs[0] * _sigmoid(accs[1])]


def _rope_epilogue(accs, ex_refs, rows, j, *, tn):
    cos_ref, sin_ref = ex_refs
    z = accs[0]
    rotated = j < (Q_WIDTH + KV_WIDTH) // tn
    cos_t = jnp.where(rotated, cos_ref[rows, :], 1.0)
    sin_t = jnp.where(rotated, sin_ref[rows, :], 0.0)
    lane = lax.broadcasted_iota(jnp.int32, cos_t.shape, 1)
    half = ROT_DIM // 2
    outs = []
    for c in range(tn // HEAD_DIM):
        x = z[:, c * HEAD_DIM:(c + 1) * HEAD_DIM]
        partner = jnp.where(lane < half, pltpu.roll(x, HEAD_DIM - half, 1), pltpu.roll(x, half, 1))
        outs.append(x * cos_t + partner * sin_t)
    return [jnp.concatenate(outs, axis=1)]


def _merge_epilogue(accs, ex_refs, rows, j):
    gate_a, br_a, gate_b, br_b = accs
    return [_sigmoid(gate_a) * br_a + _sigmoid(gate_b) * br_b]


def _residual_epilogue(accs, ex_refs, rows, j):
    return [ex_refs[0][rows, :] + accs[0]]


def _ln_swish(y, g_ref, b_ref):
    mu = jnp.mean(y, axis=-1, keepdims=True)
    yc = y - mu
    var = jnp.mean(yc * yc, axis=-1, keepdims=True)
    t = yc * lax.rsqrt(var + EPS) * g_ref[...] + b_ref[...]
    return t * _sigmoid(t)


CONV_ROWS = 32


def _conv_row_chunk(rc, xx_ref, sh_ref, w_ref, b_ref, y_ref):
    lanes = 256
    base = CONV_HALO - (CONV_K - 1)
    r0 = rc * CONV_ROWS
    for cc in range(CONV_WIDTH // lanes):
        cs = slice(cc * lanes, (cc + 1) * lanes)
        acc = jnp.zeros((CONV_ROWS, lanes), F32) + b_ref[:, cs]
        for k in range(CONV_K):
            a, b = divmod(base + k, SUBLANES)
            start = r0 + a * SUBLANES
            if b == 0:
                x = xx_ref[start:start + CONV_ROWS, cs]
            else:
                x = sh_ref[b - 1, start:start + CONV_ROWS, cs]
            acc = acc + w_ref[k:k + 1, cs] * x
        y_ref[r0:r0 + CONV_ROWS, cs] = acc


def _attn_unit(q_ref, k_parts, v_parts, sink_ref, o_ref, q_rows, kh, mask):
    ks = slice(kh * HEAD_DIM, (kh + 1) * HEAD_DIM)
    kk = jnp.concatenate([part[:, ks] for part in k_parts], axis=0).astype(BF16)
    vv = jnp.concatenate([part[:, ks] for part in v_parts], axis=0).astype(BF16)
    heads = [kh * GROUP + g for g in range(GROUP)]
    qg = jnp.concatenate([q_ref[q_rows, h * HEAD_DIM:(h + 1) * HEAD_DIM] for h in heads], axis=0).astype(BF16)
    sink = jnp.concatenate([jnp.broadcast_to(sink_ref[h:h + 1, 0:1], (BLOCK, 1)) for h in heads], axis=0)
    s = lax.dot_general(qg, kk, (((1,), (1,)), ((), ())), preferred_element_type=F32) * SCALE
    p = _softmax_with_sink(jnp.where(mask, s, -jnp.inf), sink)
    o = jnp.dot(p.astype(BF16), vv, preferred_element_type=F32)
    for g, h in enumerate(heads):
        o_ref[q_rows, h * HEAD_DIM:(h + 1) * HEAD_DIM] = o[g * BLOCK:(g + 1) * BLOCK].astype(o_ref.dtype)


def _branches_prompt_body(cur_ref, prev_ref, w_ref, b_ref, g_ref, beta_ref,
                          q_ref, kc_ref, kp_ref, vc_ref, vp_ref, sink_ref,
                          c_ref, o_ref, xx_ref, sh_ref, y_ref, *, tl):
    first = pl.program_id(1) == 0
    xx_ref[0:CONV_HALO, :] = jnp.where(first, 0.0, prev_ref[...])
    xx_ref[CONV_HALO:, :] = cur_ref[...]
    n_sh = sh_ref.shape[1]
    for b in range(1, SUBLANES):
        sh_ref[b - 1] = xx_ref[b:b + n_sh, :]

    rows = GROUP * BLOCK
    r = lax.broadcasted_iota(jnp.int32, (rows, 2 * BLOCK), 0) % BLOCK
    c = lax.broadcasted_iota(jnp.int32, (rows, 2 * BLOCK), 1)
    diff = (r + BLOCK) - c
    band = (diff >= 0) & (diff < WINDOW)
    band_first = band & ((c >= BLOCK) | jnp.logical_not(first))

    units = []
    for blk in range(tl // BLOCK):
        q_rows = slice(blk * BLOCK, (blk + 1) * BLOCK)
        if blk == 0:
            k_parts, v_parts, mask = (kp_ref, kc_ref.at[q_rows]), (vp_ref, vc_ref.at[q_rows]), band_first
        else:
            kv_rows = slice((blk - 1) * BLOCK, (blk + 1) * BLOCK)
            k_parts, v_parts, mask = (kc_ref.at[kv_rows],), (vc_ref.at[kv_rows],), band
        units += [(q_rows, kh, k_parts, v_parts, mask) for kh in range(N_KV_HEADS)]

    n_chunks = tl // CONV_ROWS
    assert len(units) == n_chunks
    always = pl.program_id(1) < pl.num_programs(1)
    for rc, (q_rows, kh, k_parts, v_parts, mask) in enumerate(units):
        @pl.when(always)
        def _pair(rc=rc, q_rows=q_rows, kh=kh, k_parts=k_parts, v_parts=v_parts, mask=mask):
            _conv_row_chunk(rc, xx_ref, sh_ref, w_ref, b_ref, y_ref)
            _attn_unit(q_ref, k_parts, v_parts, sink_ref, o_ref, q_rows, kh, mask)
    c_ref[...] = _ln_swish(y_ref[...], g_ref, beta_ref).astype(c_ref.dtype)


def _branches_prompt(u, qkv, conv_w, conv_b, ln_g, ln_b, sinks, *, n_batch, seq, tl):
    c = CONV_WIDTH
    nl = seq // tl
    halo_per_tile = tl // CONV_HALO
    blocks_per_tile = tl // BLOCK
    k_col = Q_WIDTH // KV_WIDTH
    row = lambda a: a.reshape(1, c)
    vec = pl.BlockSpec((1, c), lambda b, l: (0, 0))
    tile = lambda width, col: pl.BlockSpec((tl, width), lambda b, l: (b * nl + l, col))
    prev_block = lambda col: pl.BlockSpec(
        (BLOCK, KV_WIDTH), lambda b, l: (jnp.maximum((b * nl + l) * blocks_per_tile - 1, 0), col))
    return pl.pallas_call(
        functools.partial(_branches_prompt_body, tl=tl),
        grid=(n_batch, nl),
        in_specs=[tile(c, 0),
                  pl.BlockSpec((CONV_HALO, c),
                               lambda b, l: (jnp.maximum((b * nl + l) * halo_per_tile - 1, 0), 0)),
                  pl.BlockSpec((CONV_K, c), lambda b, l: (0, 0)),
                  vec, vec, vec,
                  tile(Q_WIDTH, 0),
                  tile(KV_WIDTH, k_col), prev_block(k_col), tile(KV_WIDTH, k_col + 1), prev_block(k_col + 1),
                  pl.BlockSpec((N_HEADS, HEAD_DIM), lambda b, l: (0, 0))],
        out_specs=[tile(c, 0), tile(Q_WIDTH, 0)],
        out_shape=[jax.ShapeDtypeStruct((n_batch * seq, c), BF16),
                   jax.ShapeDtypeStruct((n_batch * seq, Q_WIDTH), BF16)],
        scratch_shapes=[pltpu.VMEM((tl + CONV_HALO, c), F32),
                        pltpu.VMEM((SUBLANES - 1, tl + CONV_HALO - SUBLANES, c), F32),
                        pltpu.VMEM((tl, c), F32)],
        compiler_params=_params(2),
        name="branches_prompt",
    )(u, u, conv_w, row(conv_b), row(ln_g), row(ln_b), qkv, qkv, qkv, qkv, qkv, sinks)


def _conv_a_sample_body(hist_ref, u_ref, w_ref, b_ref, g_ref, beta_ref, o_ref, new_ref):
    u = u_ref[...]
    y = u * w_ref[CONV_K - 1:CONV_K, :] + b_ref[...]
    for k in range(CONV_K - 1):
        y = y + hist_ref[k] * w_ref[k:k + 1, :]
    o_ref[...] = _ln_swish(y, g_ref, beta_ref).astype(o_ref.dtype)
    new_ref[0:CONV_K - 2] = hist_ref[1:CONV_K - 1]
    new_ref[CONV_K - 2] = u


def _conv_a_sample(hist_t, u, conv_w, conv_b, ln_g, ln_b, *, bs):
    n, c = u.shape
    row = lambda a: a.reshape(1, c)
    vec = pl.BlockSpec((1, c), lambda b: (0, 0))
    hist_spec = pl.BlockSpec((CONV_K - 1, bs, c), lambda b: (0, b, 0))
    return pl.pallas_call(
        _conv_a_sample_body,
        grid=(n // bs,),
        in_specs=[hist_spec,
                  pl.BlockSpec((bs, c), lambda b: (b, 0)),
                  pl.BlockSpec((CONV_K, c), lambda b: (0, 0)),
                  vec, vec, vec],
        out_specs=[pl.BlockSpec((bs, c), lambda b: (b, 0)), hist_spec],
        out_shape=[jax.ShapeDtypeStruct((n, c), BF16), jax.ShapeDtypeStruct(hist_t.shape, F32)],
        compiler_params=_params(1),
        name="conv_a_sample",
    )(hist_t, u, conv_w, row(conv_b), row(ln_g), row(ln_b))


def _softmax_with_sink(s, sink):
    m = jnp.maximum(jnp.max(s, axis=-1, keepdims=True), sink)
    p = jnp.exp(s - m)
    return p / (jnp.sum(p, axis=-1, keepdims=True) + jnp.exp(sink - m))


def _attn_sample_body(qkv_ref, ck_ref, cv_ref, sink_ref, o_ref, ko_ref, vo_ref, *, bs):
    n_rows = WINDOW * N_KV_HEADS
    keep = n_rows - N_KV_HEADS
    k_row = N_HEADS
    v_row = N_HEADS + N_KV_HEADS
    head = lax.broadcasted_iota(jnp.int32, (N_HEADS, n_rows), 0)
    col = lax.broadcasted_iota(jnp.int32, (N_HEADS, n_rows), 1)
    same_group = (col % N_KV_HEADS) == (head // GROUP)
    sink = sink_ref[:, 0:1]

    def per_sample(b, carry):
        ko_ref[b, 0:keep, :] = ck_ref[b, N_KV_HEADS:n_rows, :]
        ko_ref[b, keep:n_rows, :] = qkv_ref[b, k_row:k_row + N_KV_HEADS, :]
        vo_ref[b, 0:keep, :] = cv_ref[b, N_KV_HEADS:n_rows, :]
        vo_ref[b, keep:n_rows, :] = qkv_ref[b, v_row:v_row + N_KV_HEADS, :]
        q = qkv_ref[b, 0:N_HEADS, :].astype(BF16)
        kk = ko_ref[b].astype(BF16)
        vv = vo_ref[b].astype(BF16)
        s = lax.dot_general(q, kk, (((1,), (1,)), ((), ())), preferred_element_type=F32) * SCALE
        p = _softmax_with_sink(jnp.where(same_group, s, -jnp.inf), sink)
        o_ref[b] = jnp.dot(p.astype(BF16), vv, preferred_element_type=F32).astype(o_ref.dtype)
        return carry

    lax.fori_loop(0, bs, per_sample, 0)


def _attn_sample(qkv, cache_k, cache_v, sinks, *, bs):
    n = qkv.shape[0]
    n_rows = QKV_WIDTH // HEAD_DIM
    cache = pl.BlockSpec((bs, WINDOW * N_KV_HEADS, HEAD_DIM), lambda b: (b, 0, 0))
    o, k_new, v_new = pl.pallas_call(
        functools.partial(_attn_sample_body, bs=bs),
        grid=(n // bs,),
        in_specs=[pl.BlockSpec((bs, n_rows, HEAD_DIM), lambda b: (b, 0, 0)), cache, cache,
                  pl.BlockSpec((N_HEADS, HEAD_DIM), lambda b: (0, 0))],
        out_specs=[pl.BlockSpec((bs, N_HEADS, HEAD_DIM), lambda b: (b, 0, 0)), cache, cache],
        out_shape=[jax.ShapeDtypeStruct((n, N_HEADS, HEAD_DIM), BF16),
                   jax.ShapeDtypeStruct(cache_k.shape, F32), jax.ShapeDtypeStruct(cache_v.shape, F32)],
        compiler_params=_params(1),
        name="attn_sample",
    )(qkv.reshape(n, n_rows, HEAD_DIM), cache_k, cache_v, sinks)
    return o.reshape(n, Q_WIDTH), k_new, v_new


def _ffn_body(*refs, tm, n_split, tiles_per_seq, up_col_block0):
    xp_refs = refs[:n_split]
    (xs_ref, wg_ref, wv_ref, cwg_ref, cwv_ref, cbg_ref, cbv_ref, hg_ref, hv_ref,
     fp_ref, tg_ref, tv_ref, fs_ref, ng_ref, nv_ref,
     wg_fetch, wv_fetch, wg_bf, wv_bf, eg0_ref, eg1_ref, ev0_ref, ev1_ref, sems) = refs[n_split:]
    i = pl.program_id(1)
    chunks = _row_chunks(tm)
    rc = chunks[0].stop
    block_rows = tm // n_split
    assert len(chunks) % 2 == 0 and block_rows % rc == 0
    stage_bufs = ((eg0_ref, ev0_ref), (eg1_ref, ev1_ref))
    last_bufs = stage_bufs[(len(chunks) - 1) % 2]
    head = slice(0, SUBLANES)
    tail = slice(rc, rc + SUBLANES)

    @pl.when(i == 0)
    def _new_column_block():
        _advance_weights(pl.program_id(0), pl.num_programs(0), (wg_ref, wv_ref), ((0, 0), (0, up_col_block0)),
                         (wg_fetch, wv_fetch), (wg_bf, wv_bf), sems)
        x = xs_ref[...]
        ys = []
        for w_bf, cw_ref, cb_ref, h_ref, n_ref in ((wg_bf, cwg_ref, cbg_ref, hg_ref, ng_ref),
                                                   (wv_bf, cwv_ref, cbv_ref, hv_ref, nv_ref)):
            d = jnp.dot(x, w_bf[...], preferred_element_type=F32)
            h0 = h_ref[:, 0, :]
            h1 = h_ref[:, 1, :]
            n_ref[:, 0, :] = h1
            n_ref[:, 1, :] = d
            ys.append(cw_ref[0:1, :] * h0 + cw_ref[1:2, :] * h1 + cw_ref[2:3, :] * d + cb_ref[...])
        fs_ref[...] = (ys[0] * _sigmoid(ys[0]) * ys[1]).astype(fs_ref.dtype)

    @pl.when(i % tiles_per_seq == 0)
    def _zero_history():
        for e_ref in last_bufs:
            e_ref[tail, :] = jnp.zeros((SUBLANES, e_ref.shape[1]), F32)

    def conv(e_ref, cw_ref, cb_ref):
        y = cb_ref[...]
        for k in range(FFN_CONV_K):
            r0 = SUBLANES - (FFN_CONV_K - 1) + k
            y = y + cw_ref[k:k + 1, :] * e_ref[r0:r0 + rc, :]
        return y

    for s in range(len(chunks) + 1):
        if s < len(chunks):
            blk, r0 = divmod(chunks[s].start, block_rows)
            x = xp_refs[blk][r0:r0 + rc, :]
            for e_ref, e_prev, w_bf in zip(stage_bufs[s % 2], stage_bufs[(s - 1) % 2], (wg_bf, wv_bf)):
                e_ref[head, :] = e_prev[tail, :]
                e_ref[SUBLANES:, :] = jnp.dot(x, w_bf[...], preferred_element_type=F32)
        if s >= 1:
            eg_ref, ev_ref = stage_bufs[(s - 1) % 2]
            yg = conv(eg_ref, cwg_ref, cbg_ref)
            yv = conv(ev_ref, cwv_ref, cbv_ref)
            fp_ref[chunks[s - 1], :] = (yg * _sigmoid(yg) * yv).astype(fp_ref.dtype)
    for e_ref, t_ref in zip(last_bufs, (tg_ref, tv_ref)):
        t_ref[0] = e_ref[tail, :]


def _ffn_up(x_p, x_s, w_up, conv_w, conv_b, hist, *, tm, n_split, tn, seq):
    m, d = x_p.shape
    m_s = x_s.shape[0]
    half_blocks = D_FF // tn
    tiles_per_seq = seq // tm
    col = lambda off: (lambda j, i: (0, off + j))
    conv_b = conv_b.reshape(1, 2 * D_FF)
    in_specs = [pl.BlockSpec((tm // n_split, d), lambda j, i, t=t: (n_split * i + t, 0)) for t in range(n_split)]
    in_specs += [pl.BlockSpec((m_s, d), lambda j, i: (0, 0)),
                pl.BlockSpec(memory_space=pl.ANY), pl.BlockSpec(memory_space=pl.ANY),
                pl.BlockSpec((FFN_CONV_K, tn), col(0)), pl.BlockSpec((FFN_CONV_K, tn), col(half_blocks)),
                pl.BlockSpec((1, tn), col(0)), pl.BlockSpec((1, tn), col(half_blocks))]
    n_hist = FFN_CONV_K - 1
    in_specs += [pl.BlockSpec((m_s, n_hist, tn), lambda j, i: (0, 0, j)),
                 pl.BlockSpec((m_s, n_hist, tn), lambda j, i: (0, 0, half_blocks + j))]
    args = [x_p] * n_split + [x_s, w_up, w_up, conv_w, conv_w, conv_b, conv_b, hist, hist]
    tail_spec = pl.BlockSpec((1, SUBLANES, tn), lambda j, i: (i // tiles_per_seq, 0, j))
    tail_shape = jax.ShapeDtypeStruct((m // seq, SUBLANES, D_FF), F32)
    s_spec = pl.BlockSpec((m_s, tn), lambda j, i: (0, j))
    hist_spec = pl.BlockSpec((m_s, n_hist, tn), lambda j, i: (0, 0, j))
    hist_shape = jax.ShapeDtypeStruct((m_s, n_hist, D_FF), F32)
    return pl.pallas_call(
        functools.partial(_ffn_body, tm=tm, n_split=n_split, tiles_per_seq=tiles_per_seq,
                          up_col_block0=half_blocks),
        grid=(half_blocks, m // tm),
        in_specs=in_specs,
        out_specs=[pl.BlockSpec((tm, tn), lambda j, i: (i, j)), tail_spec, tail_spec, s_spec, hist_spec, hist_spec],
        out_shape=[jax.ShapeDtypeStruct((m, D_FF), BF16), tail_shape, tail_shape,
                   jax.ShapeDtypeStruct((m_s, D_FF), BF16), hist_shape, hist_shape],
        scratch_shapes=([pltpu.VMEM((d, tn), F32)] * 2 + [pltpu.VMEM((d, tn), BF16)] * 2
                        + [pltpu.VMEM((min(tm, ROW_CHUNK) + SUBLANES, tn), F32)] * 4
                        + [pltpu.SemaphoreType.DMA((2,))]),
        compiler_params=_params(2),
        name="ffn_up",
    )(*args)


def kernel(x_prompt, x_sample, cache_k_win, cache_v_win, state_conv_a, state_conv_ffn, norm1_g, w_in,
           conv_a_w, conv_a_b, ln_a_g, ln_a_b, w_a_out, attn_sinks, w_attn_out, w_o, norm2_g, w_up,
           conv_f_w, conv_f_b, w_down, norm_f_g):
    n_batch, seq, d = x_prompt.shape
    n_dec = x_sample.shape[0]
    assert x_sample.shape[1] == 1 and w_in.shape[0] == 1
    w_in, w_a_out, w_attn_out, w_o, w_down = w_in[0], w_a_out[0], w_attn_out[0], w_o[0], w_down[0]
    sinks = jnp.broadcast_to(attn_sinks[0][:, None], (N_HEADS, HEAD_DIM))
    conv_a_args = (conv_a_w[0], conv_a_b[0], ln_a_g[0], ln_a_b[0])
    tm, tr = 1024, 256
    tile = lambda j, i: (i, j)
    full = lambda arrs: (arrs, 0, arrs[0].shape[1])

    x = (x_prompt.reshape(n_batch * seq, d), x_sample.reshape(n_dec, d))
    xn = tuple(_rmsnorm(a, norm1_g[0], BF16, tr) for a in x)
    (u,) = _ws_matmul([full(xn)], [(0, w_in, 0), (0, w_in, CONV_WIDTH)], [], _glu_epilogue, [F32],
                      n_cols=CONV_WIDTH, tm=tm, tn=512, name="in_glu")
    cos_p, sin_p = _rope_tables(seq, 0, 1)
    cos_s, sin_s = _rope_tables(n_dec, PAST_LEN, 0)
    rope_map = lambda j, i: (i % (seq // tm), 0)
    tn_qkv = KV_WIDTH
    (qkv,) = _ws_matmul([full(xn)], [(0, w_in, A_END)],
                        [((cos_p, cos_s), HEAD_DIM, rope_map), ((sin_p, sin_s), HEAD_DIM, rope_map)],
                        functools.partial(_rope_epilogue, tn=tn_qkv), [F32],
                        n_cols=QKV_WIDTH, tm=tm, tn=tn_qkv, name="in_qkv_rope")

    c_p, o_p = _branches_prompt(u[0], qkv[0], *conv_a_args, sinks, n_batch=n_batch, seq=seq, tl=256)
    hist_a = jnp.transpose(state_conv_a[0], (1, 0, 2))
    c_s, hist_a_new = _conv_a_sample(hist_a, u[1], *conv_a_args, bs=8)
    cache_shape = (n_dec, WINDOW * N_KV_HEADS, HEAD_DIM)
    o_s, k_win_s, v_win_s = _attn_sample(qkv[1], cache_k_win[0].reshape(cache_shape),
                                         cache_v_win[0].reshape(cache_shape), sinks, bs=16)

    (merged,) = _ws_matmul([full(xn), full((c_p, c_s)), full((o_p, o_s))],
                           [(0, w_in, V_END), (1, w_a_out, 0), (0, w_in, V_END + D_MODEL), (2, w_attn_out, 0)],
                           [], _merge_epilogue, [BF16],
                           n_cols=D_MODEL, tm=256, tn=512, name="gated_merge")
    (h,) = _ws_matmul([full(merged)], [(0, w_o, 0)], [(x, 512, tile)], _residual_epilogue, [F32],
                      n_cols=D_MODEL, tm=tm, tn=512, name="out_proj")

    hn = tuple(_rmsnorm(a, norm2_g[0], BF16, tr) for a in h)
    f_p, tail_g, tail_v, f_s, hist_g, hist_v = _ffn_up(hn[0], hn[1], w_up[0], conv_f_w[0], conv_f_b[0],
                                                       state_conv_ffn[0],
                                                       tm=2 * tm, n_split=2, tn=256, seq=seq)
    y = h
    k_half = D_FF // 2
    for kb in range(2):
        (y,) = _ws_matmul([((f_p, f_s), kb, k_half)], [(0, w_down, 0)], [(y, 512, tile)], _residual_epilogue,
                          [F32], n_cols=D_MODEL, tm=tm, tn=512, name="ffn_down")
    y_p, y_s = (_rmsnorm(a, norm_f_g, F32, tr) for a in y)

    qkv_p = qkv[0].reshape(n_batch, seq, QKV_WIDTH)
    win_shape = (1, n_batch, WINDOW, N_KV_HEADS, HEAD_DIM)
    k_win_p = qkv_p[:, seq - WINDOW:, Q_WIDTH:Q_WIDTH + KV_WIDTH].reshape(win_shape)
    v_win_p = qkv_p[:, seq - WINDOW:, Q_WIDTH + KV_WIDTH:].reshape(win_shape)
    conv_a_p = u[0].reshape(n_batch, seq, CONV_WIDTH)[None, :, seq - (CONV_K - 1):]
    keep = SUBLANES - (FFN_CONV_K - 1)
    conv_ffn_p = jnp.concatenate([tail_g[:, keep:], tail_v[:, keep:]], axis=-1)[None]
    conv_a_s = jnp.transpose(hist_a_new, (1, 0, 2))[None]
    conv_ffn_s = jnp.concatenate([hist_g, hist_v], axis=-1)[None]
    return (y_p.reshape(n_batch, seq, d), y_s.reshape(n_dec, 1, d), k_win_p, v_win_p, conv_a_p, conv_ffn_p,
            k_win_s.reshape(cache_k_win.shape), v_win_s.reshape(cache_v_win.shape), conv_a_s, conv_ffn_s)
```

```python
import functools

import jax
import jax.numpy as jnp
from jax import lax
from jax.experimental import pallas as pl
from jax.experimental.pallas import tpu as pltpu

F32 = jnp.float32
BF16 = jnp.bfloat16

D_MODEL = 4096
CONV_WIDTH = D_MODEL // 2
CONV_K = 31
HEAD_DIM = 128
N_HEADS = (D_MODEL // 2) // HEAD_DIM
N_KV_HEADS = N_HEADS // 4
GROUP = N_HEADS // N_KV_HEADS
ROT_DIM = HEAD_DIM // 4
ROPE_THETA = 500000.0
WINDOW = 128
BLOCK = 128
D_FF = 11008
FFN_CONV_K = 3
EPS = 1e-6
PAST_LEN = 8192
Q_WIDTH = N_HEADS * HEAD_DIM
KV_WIDTH = N_KV_HEADS * HEAD_DIM
A_END = 2 * CONV_WIDTH
V_END = A_END + Q_WIDTH + 2 * KV_WIDTH
QKV_WIDTH = Q_WIDTH + 2 * KV_WIDTH
SCALE = HEAD_DIM ** -0.5

V7X_VMEM_BYTES = 64 * 1024 * 1024
VMEM_LIMIT = V7X_VMEM_BYTES - 8 * 1024 * 1024
SUBLANES = 8
LANES = 128
CONV_HALO = 32
ROW_CHUNK = 128


def _params(n_axes):
    return pltpu.CompilerParams(dimension_semantics=("arbitrary",) * n_axes,
                                vmem_limit_bytes=VMEM_LIMIT)


def _sigmoid(x):
    return 1.0 / (1.0 + jnp.exp(-x))


def _row_chunks(tm):
    rc = min(tm, ROW_CHUNK)
    return [slice(r * rc, (r + 1) * rc) for r in range(tm // rc)]


def _rmsnorm_body(x_ref, g_ref, o_ref):
    x = x_ref[...]
    ms = jnp.mean(x * x, axis=-1, keepdims=True)
    o_ref[...] = (x * lax.rsqrt(ms + EPS) * g_ref[...]).astype(o_ref.dtype)


def _rmsnorm(x, g, out_dtype, tr):
    m, d = x.shape
    tr = min(tr, m)
    return pl.pallas_call(
        _rmsnorm_body,
        grid=(m // tr,),
        in_specs=[pl.BlockSpec((tr, d), lambda i: (i, 0)),
                  pl.BlockSpec((1, d), lambda i: (0, 0))],
        out_specs=pl.BlockSpec((tr, d), lambda i: (i, 0)),
        out_shape=jax.ShapeDtypeStruct((m, d), out_dtype),
        compiler_params=_params(1),
        name="rmsnorm",
    )(x, g.reshape(1, d))


def _rope_table_body(inv_ref, sign_ref, cos_ref, sin_ref, *, pos0, pos_step):
    rows = cos_ref.shape[0]
    i = pl.program_id(0)
    r = lax.broadcasted_iota(jnp.int32, (rows, HEAD_DIM), 0) + i * rows
    lane = lax.broadcasted_iota(jnp.int32, (rows, HEAD_DIM), 1)
    pos = (pos0 + pos_step * r).astype(F32)
    ang = pos * inv_ref[...]
    cos_ref[...] = jnp.where(lane < ROT_DIM, jnp.cos(ang), 1.0)
    sin_ref[...] = sign_ref[...] * jnp.sin(ang)


def _rope_tables(n_rows, pos0, pos_step):
    half = ROT_DIM // 2
    inv = ROPE_THETA ** (-2.0 * jnp.arange(half, dtype=F32) / ROT_DIM)
    zeros = jnp.zeros((HEAD_DIM - ROT_DIM,), F32)
    inv_full = jnp.concatenate([inv, inv, zeros]).reshape(1, HEAD_DIM)
    sign = jnp.concatenate([-jnp.ones((half,), F32), jnp.ones((half,), F32), zeros]).reshape(1, HEAD_DIM)
    tr = min(n_rows, 512)
    spec_c = pl.BlockSpec((1, HEAD_DIM), lambda i: (0, 0))
    spec_t = pl.BlockSpec((tr, HEAD_DIM), lambda i: (i, 0))
    return pl.pallas_call(
        functools.partial(_rope_table_body, pos0=pos0, pos_step=pos_step),
        grid=(n_rows // tr,),
        in_specs=[spec_c, spec_c],
        out_specs=[spec_t, spec_t],
        out_shape=[jax.ShapeDtypeStruct((n_rows, HEAD_DIM), F32)] * 2,
        compiler_params=_params(1),
        name="rope_tables",
    )(inv_full, sign)


def _weight_copy(w_hbm, fetch_ref, sem, row0, col0):
    ks, tn = fetch_ref.shape
    return pltpu.make_async_copy(w_hbm.at[pl.ds(row0, ks), pl.ds(col0, tn)], fetch_ref, sem)


def _advance_weights(j, n_col_blocks, w_refs, geoms, fetch_refs, wbf_refs, sems):
    def copies(col_block):
        out = []
        for d, (w_hbm, fetch_ref, (row0, col_block0)) in enumerate(zip(w_refs, fetch_refs, geoms)):
            col0 = pl.multiple_of((col_block0 + col_block) * fetch_ref.shape[1], LANES)
            out.append(_weight_copy(w_hbm, fetch_ref, sems.at[d], row0, col0))
        return out

    @pl.when(j == 0)
    def _fetch_first():
        for c in copies(0):
            c.start()

    for c in copies(j):
        c.wait()
    for fetch_ref, s_ref in zip(fetch_refs, wbf_refs):
        s_ref[...] = fetch_ref[...].astype(BF16)

    @pl.when(j + 1 < n_col_blocks)
    def _fetch_next():
        for c in copies(j + 1):
            c.start()


def _ws_body(*refs, dot_lhs, w_geoms, n_lhs, n_extra, n_out, tms, epilogue):
    n_dots = len(dot_lhs)
    pos = 0
    def take(n):
        nonlocal pos
        out = refs[pos:pos + n]
        pos += n
        return out
    lhs_p, lhs_s = take(n_lhs), take(n_lhs)
    w_refs = take(n_dots)
    ex_p, ex_s = take(n_extra), take(n_extra)
    out_p, out_s = take(n_out), take(n_out)
    fetch_refs = take(n_dots)
    wbf_refs = take(n_dots)
    stage_bufs = (take(n_dots), take(n_dots))
    (sems,) = take(1)
    j = pl.program_id(0)
    i = pl.program_id(1)

    def run_rows(lhs_refs, ex_refs, out_refs, tm):
        chunks = _row_chunks(tm)
        n = chunks[0].stop
        for s in range(len(chunks) + 1):
            if s < len(chunks):
                for li, s_ref, r_ref in zip(dot_lhs, wbf_refs, stage_bufs[s % 2]):
                    r_ref[0:n, :] = jnp.dot(lhs_refs[li][chunks[s], :], s_ref[...], preferred_element_type=F32)
            if s >= 1:
                rows = chunks[s - 1]
                accs = [r_ref[0:n, :] for r_ref in stage_bufs[(s - 1) % 2]]
                for o_ref, val in zip(out_refs, epilogue(accs, ex_refs, rows, j)):
                    o_ref[rows, :] = val.astype(o_ref.dtype)

    @pl.when(i == 0)
    def _new_column_block():
        _advance_weights(j, pl.num_programs(0), w_refs, w_geoms, fetch_refs, wbf_refs, sems)
        run_rows(lhs_s, ex_s, out_s, tms[1])

    run_rows(lhs_p, ex_p, out_p, tms[0])


def _ws_matmul(lhs, dots, extras, epilogue, out_dtypes, *, n_cols, tm, tn, name):
    m_p = lhs[0][0][0].shape[0]
    m_s = lhs[0][0][1].shape[0]
    in_specs, args = [], []
    for grp in (0, 1):
        for arrs, kb, ks in lhs:
            if grp == 0:
                in_specs.append(pl.BlockSpec((tm, ks), lambda j, i, kb=kb: (i, kb)))
            else:
                in_specs.append(pl.BlockSpec((m_s, ks), lambda j, i, kb=kb: (0, kb)))
            args.append(arrs[grp])
    w_geoms = []
    for li, w, off in dots:
        _, kb, ks = lhs[li]
        in_specs.append(pl.BlockSpec(memory_space=pl.ANY))
        args.append(w)
        w_geoms.append((kb * ks, off // tn))
    for grp in (0, 1):
        for arrs, cols, imap in extras:
            if grp == 0:
                in_specs.append(pl.BlockSpec((tm, cols), imap))
            else:
                in_specs.append(pl.BlockSpec((m_s, cols), (lambda j, i: (0, j)) if cols == tn else (lambda j, i: (0, 0))))
            args.append(arrs[grp])
    out_specs = ([pl.BlockSpec((tm, tn), lambda j, i: (i, j)) for _ in out_dtypes]
                 + [pl.BlockSpec((m_s, tn), lambda j, i: (0, j)) for _ in out_dtypes])
    out_shape = ([jax.ShapeDtypeStruct((m_p, n_cols), dt) for dt in out_dtypes]
                 + [jax.ShapeDtypeStruct((m_s, n_cols), dt) for dt in out_dtypes])
    body = functools.partial(_ws_body, dot_lhs=tuple(l for l, _, _ in dots), w_geoms=tuple(w_geoms),
                             n_lhs=len(lhs), n_extra=len(extras), n_out=len(out_dtypes), tms=(tm, m_s),
                             epilogue=epilogue)
    outs = pl.pallas_call(
        body,
        grid=(n_cols // tn, m_p // tm),
        in_specs=in_specs,
        out_specs=out_specs,
        out_shape=out_shape,
        scratch_shapes=([pltpu.VMEM((lhs[li][2], tn), F32) for li, _, _ in dots]
                        + [pltpu.VMEM((lhs[li][2], tn), BF16) for li, _, _ in dots]
                        + [pltpu.VMEM((min(tm, ROW_CHUNK), tn), F32) for _ in range(2 * len(dots))]
                        + [pltpu.SemaphoreType.DMA((len(dots),))]),
        compiler_params=_params(2),
        name=name,
    )(*args)
    n = len(out_dtypes)
    return list(zip(outs[:n], outs[n:]))


def _glu_epilogue(accs, ex_refs, rows, j):
    return [accs[0] * _sigmoid(accs[1])]


def _rope_epilogue(accs, ex_refs, rows, j, *, tn):
    cos_ref, sin_ref = ex_refs
    z = accs[0]
    rotated = j < (Q_WIDTH + KV_WIDTH) // tn
    cos_t = jnp.where(rotated, cos_ref[rows, :], 1.0)
    sin_t = jnp.where(rotated, sin_ref[rows, :], 0.0)
    lane = lax.broadcasted_iota(jnp.int32, cos_t.shape, 1)
    half = ROT_DIM // 2
    outs = []
    for c in range(tn // HEAD_DIM):
        x = z[:, c * HEAD_DIM:(c + 1) * HEAD_DIM]
        partner = jnp.where(lane < half, pltpu.roll(x, HEAD_DIM - half, 1), pltpu.roll(x, half, 1))
        outs.append(x * cos_t + partner * sin_t)
    return [jnp.concatenate(outs, axis=1)]


def _merge_epilogue(accs, ex_refs, rows, j):
    gate_a, br_a, gate_b, br_b = accs
    return [_sigmoid(gate_a) * br_a + _sigmoid(gate_b) * br_b]


def _residual_epilogue(accs, ex_refs, rows, j):
    return [ex_refs[0][rows, :] + accs[0]]


def _ln_swish(y, g_ref, b_ref):
    mu = jnp.mean(y, axis=-1, keepdims=True)
    yc = y - mu
    var = jnp.mean(yc * yc, axis=-1, keepdims=True)
    t = yc * lax.rsqrt(var + EPS) * g_ref[...] + b_ref[...]
    return t * _sigmoid(t)


CONV_ROWS = 32


def _conv_row_chunk(rc, xx_ref, sh_ref, w_ref, b_ref, y_ref):
    lanes = 256
    base = CONV_HALO - (CONV_K - 1)
    r0 = rc * CONV_ROWS
    for cc in range(CONV_WIDTH // lanes):
        cs = slice(cc * lanes, (cc + 1) * lanes)
        acc = jnp.zeros((CONV_ROWS, lanes), F32) + b_ref[:, cs]
        for k in range(CONV_K):
            a, b = divmod(base + k, SUBLANES)
            start = r0 + a * SUBLANES
            if b == 0:
                x = xx_ref[start:start + CONV_ROWS, cs]
            else:
                x = sh_ref[b - 1, start:start + CONV_ROWS, cs]
            acc = acc + w_ref[k:k + 1, cs] * x
        y_ref[r0:r0 + CONV_ROWS, cs] = acc


def _attn_unit(q_ref, k_parts, v_parts, sink_ref, o_ref, q_rows, kh, mask):
    ks = slice(kh * HEAD_DIM, (kh + 1) * HEAD_DIM)
    kk = jnp.concatenate([part[:, ks] for part in k_parts], axis=0).astype(BF16)
    vv = jnp.concatenate([part[:, ks] for part in v_parts], axis=0).astype(BF16)
    heads = [kh * GROUP + g for g in range(GROUP)]
    qg = jnp.concatenate([q_ref[q_rows, h * HEAD_DIM:(h + 1) * HEAD_DIM] for h in heads], axis=0).astype(BF16)
    sink = jnp.concatenate([jnp.broadcast_to(sink_ref[h:h + 1, 0:1], (BLOCK, 1)) for h in heads], axis=0)
    s = lax.dot_general(qg, kk, (((1,), (1,)), ((), ())), preferred_element_type=F32) * SCALE
    p = _softmax_with_sink(jnp.where(mask, s, -jnp.inf), sink)
    o = jnp.dot(p.astype(BF16), vv, preferred_element_type=F32)
    for g, h in enumerate(heads):
        o_ref[q_rows, h * HEAD_DIM:(h + 1) * HEAD_DIM] = o[g * BLOCK:(g + 1) * BLOCK].astype(o_ref.dtype)


def _branches_prompt_body(cur_ref, prev_ref, w_ref, b_ref, g_ref, beta_ref,
                          q_ref, kc_ref, kp_ref, vc_ref, vp_ref, sink_ref,
                          c_ref, o_ref, xx_ref, sh_ref, y_ref, *, tl):
    first = pl.program_id(1) == 0
    xx_ref[0:CONV_HALO, :] = jnp.where(first, 0.0, prev_ref[...])
    xx_ref[CONV_HALO:, :] = cur_ref[...]
    n_sh = sh_ref.shape[1]
    for b in range(1, SUBLANES):
        sh_ref[b - 1] = xx_ref[b:b + n_sh, :]

    rows = GROUP * BLOCK
    r = lax.broadcasted_iota(jnp.int32, (rows, 2 * BLOCK), 0) % BLOCK
    c = lax.broadcasted_iota(jnp.int32, (rows, 2 * BLOCK), 1)
    diff = (r + BLOCK) - c
    band = (diff >= 0) & (diff < WINDOW)
    band_first = band & ((c >= BLOCK) | jnp.logical_not(first))

    units = []
    for blk in range(tl // BLOCK):
        q_rows = slice(blk * BLOCK, (blk + 1) * BLOCK)
        if blk == 0:
            k_parts, v_parts, mask = (kp_ref, kc_ref.at[q_rows]), (vp_ref, vc_ref.at[q_rows]), band_first
        else:
            kv_rows = slice((blk - 1) * BLOCK, (blk + 1) * BLOCK)
            k_parts, v_parts, mask = (kc_ref.at[kv_rows],), (vc_ref.at[kv_rows],), band
        units += [(q_rows, kh, k_parts, v_parts, mask) for kh in range(N_KV_HEADS)]

    n_chunks = tl // CONV_ROWS
    assert len(units) == n_chunks
    always = pl.program_id(1) < pl.num_programs(1)
    for rc, (q_rows, kh, k_parts, v_parts, mask) in enumerate(units):
        @pl.when(always)
        def _pair(rc=rc, q_rows=q_rows, kh=kh, k_parts=k_parts, v_parts=v_parts, mask=mask):
            _conv_row_chunk(rc, xx_ref, sh_ref, w_ref, b_ref, y_ref)
            _attn_unit(q_ref, k_parts, v_parts, sink_ref, o_ref, q_rows, kh, mask)
    c_ref[...] = _ln_swish(y_ref[...], g_ref, beta_ref).astype(c_ref.dtype)


def _branches_prompt(u, qkv, conv_w, conv_b, ln_g, ln_b, sinks, *, n_batch, seq, tl):
    c = CONV_WIDTH
    nl = seq // tl
    halo_per_tile = tl // CONV_HALO
    blocks_per_tile = tl // BLOCK
    k_col = Q_WIDTH // KV_WIDTH
    row = lambda a: a.reshape(1, c)
    vec = pl.BlockSpec((1, c), lambda b, l: (0, 0))
    tile = lambda width, col: pl.BlockSpec((tl, width), lambda b, l: (b * nl + l, col))
    prev_block = lambda col: pl.BlockSpec(
        (BLOCK, KV_WIDTH), lambda b, l: (jnp.maximum((b * nl + l) * blocks_per_tile - 1, 0), col))
    return pl.pallas_call(
        functools.partial(_branches_prompt_body, tl=tl),
        grid=(n_batch, nl),
        in_specs=[tile(c, 0),
                  pl.BlockSpec((CONV_HALO, c),
                               lambda b, l: (jnp.maximum((b * nl + l) * halo_per_tile - 1, 0), 0)),
                  pl.BlockSpec((CONV_K, c), lambda b, l: (0, 0)),
                  vec, vec, vec,
                  tile(Q_WIDTH, 0),
                  tile(KV_WIDTH, k_col), prev_block(k_col), tile(KV_WIDTH, k_col + 1), prev_block(k_col + 1),
                  pl.BlockSpec((N_HEADS, HEAD_DIM), lambda b, l: (0, 0))],
        out_specs=[tile(c, 0), tile(Q_WIDTH, 0)],
        out_shape=[jax.ShapeDtypeStruct((n_batch * seq, c), BF16),
                   jax.ShapeDtypeStruct((n_batch * seq, Q_WIDTH), BF16)],
        scratch_shapes=[pltpu.VMEM((tl + CONV_HALO, c), F32),
                        pltpu.VMEM((SUBLANES - 1, tl + CONV_HALO - SUBLANES, c), F32),
                        pltpu.VMEM((tl, c), F32)],
        compiler_params=_params(2),
        name="branches_prompt",
    )(u, u, conv_w, row(conv_b), row(ln_g), row(ln_b), qkv, qkv, qkv, qkv, qkv, sinks)


def _conv_a_sample_body(hist_ref, u_ref, w_ref, b_ref, g_ref, beta_ref, o_ref, new_ref):
    u = u_ref[...]
    y = u * w_ref[CONV_K - 1:CONV_K, :] + b_ref[...]
    for k in range(CONV_K - 1):
        y = y + hist_ref[k] * w_ref[k:k + 1, :]
    o_ref[...] = _ln_swish(y, g_ref, beta_ref).astype(o_ref.dtype)
    new_ref[0:CONV_K - 2] = hist_ref[1:CONV_K - 1]
    new_ref[CONV_K - 2] = u


def _conv_a_sample(hist_t, u, conv_w, conv_b, ln_g, ln_b, *, bs):
    n, c = u.shape
    row = lambda a: a.reshape(1, c)
    vec = pl.BlockSpec((1, c), lambda b: (0, 0))
    hist_spec = pl.BlockSpec((CONV_K - 1, bs, c), lambda b: (0, b, 0))
    return pl.pallas_call(
        _conv_a_sample_body,
        grid=(n // bs,),
        in_specs=[hist_spec,
                  pl.BlockSpec((bs, c), lambda b: (b, 0)),
                  pl.BlockSpec((CONV_K, c), lambda b: (0, 0)),
                  vec, vec, vec],
        out_specs=[pl.BlockSpec((bs, c), lambda b: (b, 0)), hist_spec],
        out_shape=[jax.ShapeDtypeStruct((n, c), BF16), jax.ShapeDtypeStruct(hist_t.shape, F32)],
        compiler_params=_params(1),
        name="conv_a_sample",
    )(hist_t, u, conv_w, row(conv_b), row(ln_g), row(ln_b))


def _softmax_with_sink(s, sink):
    m = jnp.maximum(jnp.max(s, axis=-1, keepdims=True), sink)
    p = jnp.exp(s - m)
    return p / (jnp.sum(p, axis=-1, keepdims=True) + jnp.exp(sink - m))


def _attn_sample_body(qkv_ref, ck_ref, cv_ref, sink_ref, o_ref, ko_ref, vo_ref, *, bs):
    n_rows = WINDOW * N_KV_HEADS
    keep = n_rows - N_KV_HEADS
    k_row = N_HEADS
    v_row = N_HEADS + N_KV_HEADS
    head = lax.broadcasted_iota(jnp.int32, (N_HEADS, n_rows), 0)
    col = lax.broadcasted_iota(jnp.int32, (N_HEADS, n_rows), 1)
    same_group = (col % N_KV_HEADS) == (head // GROUP)
    sink = sink_ref[:, 0:1]

    def per_sample(b, carry):
        ko_ref[b, 0:keep, :] = ck_ref[b, N_KV_HEADS:n_rows, :]
        ko_ref[b, keep:n_rows, :] = qkv_ref[b, k_row:k_row + N_KV_HEADS, :]
        vo_ref[b, 0:keep, :] = cv_ref[b, N_KV_HEADS:n_rows, :]
        vo_ref[b, keep:n_rows, :] = qkv_ref[b, v_row:v_row + N_KV_HEADS, :]
        q = qkv_ref[b, 0:N_HEADS, :].astype(BF16)
        kk = ko_ref[b].astype(BF16)
        vv = vo_ref[b].astype(BF16)
        s = lax.dot_general(q, kk, (((1,), (1,)), ((), ())), preferred_element_type=F32) * SCALE
        p = _softmax_with_sink(jnp.where(same_group, s, -jnp.inf), sink)
        o_ref[b] = jnp.dot(p.astype(BF16), vv, preferred_element_type=F32).astype(o_ref.dtype)
        return carry

    lax.fori_loop(0, bs, per_sample, 0, unroll=4)


def _attn_sample(qkv, cache_k, cache_v, sinks, *, bs):
    n = qkv.shape[0]
    n_rows = QKV_WIDTH // HEAD_DIM
    cache = pl.BlockSpec((bs, WINDOW * N_KV_HEADS, HEAD_DIM), lambda b: (b, 0, 0))
    o, k_new, v_new = pl.pallas_call(
        functools.partial(_attn_sample_body, bs=bs),
        grid=(n // bs,),
        in_specs=[pl.BlockSpec((bs, n_rows, HEAD_DIM), lambda b: (b, 0, 0)), cache, cache,
                  pl.BlockSpec((N_HEADS, HEAD_DIM), lambda b: (0, 0))],
        out_specs=[pl.BlockSpec((bs, N_HEADS, HEAD_DIM), lambda b: (b, 0, 0)), cache, cache],
        out_shape=[jax.ShapeDtypeStruct((n, N_HEADS, HEAD_DIM), BF16),
                   jax.ShapeDtypeStruct(cache_k.shape, F32), jax.ShapeDtypeStruct(cache_v.shape, F32)],
        compiler_params=_params(1),
        name="attn_sample",
    )(qkv.reshape(n, n_rows, HEAD_DIM), cache_k, cache_v, sinks)
    return o.reshape(n, Q_WIDTH), k_new, v_new


def _ffn_body(xp_ref, xs_ref, wg_ref, wv_ref, cwg_ref, cwv_ref, cbg_ref, cbv_ref,
              hg_ref, hv_ref,
              fp_ref, tg_ref, tv_ref, fs_ref, ng_ref, nv_ref,
              wg_fetch, wv_fetch, wg_bf, wv_bf, eg0_ref, eg1_ref, ev0_ref, ev1_ref, sems,
              *, tm, tiles_per_seq, up_col_block0):
    i = pl.program_id(1)
    chunks = _row_chunks(tm)
    rc = chunks[0].stop
    assert len(chunks) % 2 == 0
    stage_bufs = ((eg0_ref, ev0_ref), (eg1_ref, ev1_ref))
    last_bufs = stage_bufs[(len(chunks) - 1) % 2]
    head = slice(0, SUBLANES)
    tail = slice(rc, rc + SUBLANES)

    @pl.when(i == 0)
    def _new_column_block():
        _advance_weights(pl.program_id(0), pl.num_programs(0), (wg_ref, wv_ref), ((0, 0), (0, up_col_block0)),
                         (wg_fetch, wv_fetch), (wg_bf, wv_bf), sems)
        x = xs_ref[...]
        ys = []
        for w_bf, cw_ref, cb_ref, h_ref, n_ref in ((wg_bf, cwg_ref, cbg_ref, hg_ref, ng_ref),
                                                   (wv_bf, cwv_ref, cbv_ref, hv_ref, nv_ref)):
            d = jnp.dot(x, w_bf[...], preferred_element_type=F32)
            h0 = h_ref[:, 0, :]
            h1 = h_ref[:, 1, :]
            n_ref[:, 0, :] = h1
            n_ref[:, 1, :] = d
            ys.append(cw_ref[0:1, :] * h0 + cw_ref[1:2, :] * h1 + cw_ref[2:3, :] * d + cb_ref[...])
        fs_ref[...] = (ys[0] * _sigmoid(ys[0]) * ys[1]).astype(fs_ref.dtype)

    @pl.when(i % tiles_per_seq == 0)
    def _zero_history():
        for e_ref in last_bufs:
            e_ref[tail, :] = jnp.zeros((SUBLANES, e_ref.shape[1]), F32)

    def conv(e_ref, cw_ref, cb_ref):
        y = cb_ref[...]
        for k in range(FFN_CONV_K):
            r0 = SUBLANES - (FFN_CONV_K - 1) + k
            y = y + cw_ref[k:k + 1, :] * e_ref[r0:r0 + rc, :]
        return y

    for s in range(len(chunks) + 1):
        if s < len(chunks):
            x = xp_ref[chunks[s], :]
            for e_ref, e_prev, w_bf in zip(stage_bufs[s % 2], stage_bufs[(s - 1) % 2], (wg_bf, wv_bf)):
                e_ref[head, :] = e_prev[tail, :]
                e_ref[SUBLANES:, :] = jnp.dot(x, w_bf[...], preferred_element_type=F32)
        if s >= 1:
            eg_ref, ev_ref = stage_bufs[(s - 1) % 2]
            yg = conv(eg_ref, cwg_ref, cbg_ref)
            yv = conv(ev_ref, cwv_ref, cbv_ref)
            fp_ref[chunks[s - 1], :] = (yg * _sigmoid(yg) * yv).astype(fp_ref.dtype)
    for e_ref, t_ref in zip(last_bufs, (tg_ref, tv_ref)):
        t_ref[0] = e_ref[tail, :]


def _ffn_up(x_p, x_s, w_up, conv_w, conv_b, hist, *, tm, tn, seq):
    m, d = x_p.shape
    m_s = x_s.shape[0]
    half_blocks = D_FF // tn
    tiles_per_seq = seq // tm
    col = lambda off: (lambda j, i: (0, off + j))
    conv_b = conv_b.reshape(1, 2 * D_FF)
    in_specs = [pl.BlockSpec((tm, d), lambda j, i: (i, 0)),
                pl.BlockSpec((m_s, d), lambda j, i: (0, 0)),
                pl.BlockSpec(memory_space=pl.ANY), pl.BlockSpec(memory_space=pl.ANY),
                pl.BlockSpec((FFN_CONV_K, tn), col(0)), pl.BlockSpec((FFN_CONV_K, tn), col(half_blocks)),
                pl.BlockSpec((1, tn), col(0)), pl.BlockSpec((1, tn), col(half_blocks))]
    n_hist = FFN_CONV_K - 1
    in_specs += [pl.BlockSpec((m_s, n_hist, tn), lambda j, i: (0, 0, j)),
                 pl.BlockSpec((m_s, n_hist, tn), lambda j, i: (0, 0, half_blocks + j))]
    args = [x_p, x_s, w_up, w_up, conv_w, conv_w, conv_b, conv_b, hist, hist]
    tail_spec = pl.BlockSpec((1, SUBLANES, tn), lambda j, i: (i // tiles_per_seq, 0, j))
    tail_shape = jax.ShapeDtypeStruct((m // seq, SUBLANES, D_FF), F32)
    s_spec = pl.BlockSpec((m_s, tn), lambda j, i: (0, j))
    hist_spec = pl.BlockSpec((m_s, n_hist, tn), lambda j, i: (0, 0, j))
    hist_shape = jax.ShapeDtypeStruct((m_s, n_hist, D_FF), F32)
    return pl.pallas_call(
        functools.partial(_ffn_body, tm=tm, tiles_per_seq=tiles_per_seq, up_col_block0=half_blocks),
        grid=(half_blocks, m // tm),
        in_specs=in_specs,
        out_specs=[pl.BlockSpec((tm, tn), lambda j, i: (i, j)), tail_spec, tail_spec, s_spec, hist_spec, hist_spec],
        out_shape=[jax.ShapeDtypeStruct((m, D_FF), BF16), tail_shape, tail_shape,
                   jax.ShapeDtypeStruct((m_s, D_FF), BF16), hist_shape, hist_shape],
        scratch_shapes=([pltpu.VMEM((d, tn), F32)] * 2 + [pltpu.VMEM((d, tn), BF16)] * 2
                        + [pltpu.VMEM((min(tm, ROW_CHUNK) + SUBLANES, tn), F32)] * 4
                        + [pltpu.SemaphoreType.DMA((2,))]),
        compiler_params=_params(2),
        name="ffn_up",
    )(*args)


def kernel(x_prompt, x_sample, cache_k_win, cache_v_win, state_conv_a, state_conv_ffn, norm1_g, w_in,
           conv_a_w, conv_a_b, ln_a_g, ln_a_b, w_a_out, attn_sinks, w_attn_out, w_o, norm2_g, w_up,
           conv_f_w, conv_f_b, w_down, norm_f_g):
    n_batch, seq, d = x_prompt.shape
    n_dec = x_sample.shape[0]
    assert x_sample.shape[1] == 1 and w_in.shape[0] == 1
    w_in, w_a_out, w_attn_out, w_o, w_down = w_in[0], w_a_out[0], w_attn_out[0], w_o[0], w_down[0]
    sinks = jnp.broadcast_to(attn_sinks[0][:, None], (N_HEADS, HEAD_DIM))
    conv_a_args = (conv_a_w[0], conv_a_b[0], ln_a_g[0], ln_a_b[0])
    tm, tr = 1024, 512
    tile = lambda j, i: (i, j)
    full = lambda arrs: (arrs, 0, arrs[0].shape[1])

    x = (x_prompt.reshape(n_batch * seq, d), x_sample.reshape(n_dec, d))
    xn = tuple(_rmsnorm(a, norm1_g[0], BF16, tr) for a in x)
    (u,) = _ws_matmul([full(xn)], [(0, w_in, 0), (0, w_in, CONV_WIDTH)], [], _glu_epilogue, [F32],
                      n_cols=CONV_WIDTH, tm=tm, tn=512, name="in_glu")
    cos_p, sin_p = _rope_tables(seq, 0, 1)
    cos_s, sin_s = _rope_tables(n_dec, PAST_LEN, 0)
    rope_map = lambda j, i: (i % (seq // tm), 0)
    tn_qkv = KV_WIDTH
    (qkv,) = _ws_matmul([full(xn)], [(0, w_in, A_END)],
                        [((cos_p, cos_s), HEAD_DIM, rope_map), ((sin_p, sin_s), HEAD_DIM, rope_map)],
                        functools.partial(_rope_epilogue, tn=tn_qkv), [F32],
                        n_cols=QKV_WIDTH, tm=tm, tn=tn_qkv, name="in_qkv_rope")

    c_p, o_p = _branches_prompt(u[0], qkv[0], *conv_a_args, sinks, n_batch=n_batch, seq=seq, tl=256)
    hist_a = jnp.transpose(state_conv_a[0], (1, 0, 2))
    c_s, hist_a_new = _conv_a_sample(hist_a, u[1], *conv_a_args, bs=16)
    cache_shape = (n_dec, WINDOW * N_KV_HEADS, HEAD_DIM)
    o_s, k_win_s, v_win_s = _attn_sample(qkv[1], cache_k_win[0].reshape(cache_shape),
                                         cache_v_win[0].reshape(cache_shape), sinks, bs=16)

    (merged,) = _ws_matmul([full(xn), full((c_p, c_s)), full((o_p, o_s))],
                           [(0, w_in, V_END), (1, w_a_out, 0), (0, w_in, V_END + D_MODEL), (2, w_attn_out, 0)],
                           [], _merge_epilogue, [BF16],
                           n_cols=D_MODEL, tm=256, tn=512, name="gated_merge")
    (h,) = _ws_matmul([full(merged)], [(0, w_o, 0)], [(x, 512, tile)], _residual_epilogue, [F32],
                      n_cols=D_MODEL, tm=tm, tn=512, name="out_proj")

    hn = tuple(_rmsnorm(a, norm2_g[0], BF16, tr) for a in h)
    f_p, tail_g, tail_v, f_s, hist_g, hist_v = _ffn_up(hn[0], hn[1], w_up[0], conv_f_w[0], conv_f_b[0],
                                                       state_conv_ffn[0],
                                                     tm=tm, tn=256, seq=seq)
    y = h
    k_half = D_FF // 2
    for kb in range(2):
        (y,) = _ws_matmul([((f_p, f_s), kb, k_half)], [(0, w_down, 0)], [(y, 512, tile)], _residual_epilogue,
                          [F32], n_cols=D_MODEL, tm=tm, tn=512, name="ffn_down")
    y_p, y_s = (_rmsnorm(a, norm_f_g, F32, tr) for a in y)

    qkv_p = qkv[0].reshape(n_batch, seq, QKV_WIDTH)
    win_shape = (1, n_batch, WINDOW, N_KV_HEADS, HEAD_DIM)
    k_win_p = qkv_p[:, seq - WINDOW:, Q_WIDTH:Q_WIDTH + KV_WIDTH].reshape(win_shape)
    v_win_p = qkv_p[:, seq - WINDOW:, Q_WIDTH + KV_WIDTH:].reshape(win_shape)
    conv_a_p = u[0].reshape(n_batch, seq, CONV_WIDTH)[None, :, seq - (CONV_K - 1):]
    keep = SUBLANES - (FFN_CONV_K - 1)
    conv_ffn_p = jnp.concatenate([tail_g[:, keep:], tail_v[:, keep:]], axis=-1)[None]
    conv_a_s = jnp.transpose(hist_a_new, (1, 0, 2))[None]
    conv_ffn_s = jnp.concatenate([hist_g, hist_v], axis=-1)[None]
    return (y_p.reshape(n_batch, seq, d), y_s.reshape(n_dec, 1, d), k_win_p, v_win_p, conv_a_p, conv_ffn_p,
            k_win_s.reshape(cache_k_win.shape), v_win_s.reshape(cache_v_win.shape), conv_a_s, conv_ffn_s)
```
